```python
import math
import jax
import jax.numpy as jnp
from jax import lax
import numpy as np

D_MODEL = 2048
BATCH = 4
SEQ = 4096
DEPTH = 4

CHUNK = 64
Q_BLOCK = 128
D_MIX = D_MODEL
D_RNN = D_MIX // 2
RNN_BLOCKS = 16
RNN_BLOCK = D_RNN // RNN_BLOCKS
CONV_W = 4
RG_C = 8.0
D_ATT = D_MIX - D_RNN
N_DIFF_HEADS = 8
DIFF_VDIM = D_ATT // N_DIFF_HEADS
DIFF_QK = DIFF_VDIM // 2
N_BUCKETS = 32
MAX_DIST = 128
N_EXPERTS = 64
TOP_K = 8
N_GROUPS = 8
TOPK_GROUPS = 4
D_EXPERT = D_MODEL // 4
D_SHARED = D_EXPERT
ROUTED_SCALE = 2.5
MOE_BLOCK = 128
ALPHA = (2.0 * DEPTH) ** 0.25
BETA = (8.0 * DEPTH) ** -0.25
LN_EPS = 1e-5
NEG_INF = -1e30
W_IN_COLS = 2 * D_RNN + 3 * D_ATT

kernel_name = 'hybrid_rglru_diffattn_moe_deepnorm'

F32 = jnp.float32


def layer_norm(x, g, b):
    xf = x.astype(F32)
    mu = jnp.mean(xf, -1, keepdims=True)
    var = jnp.mean(jnp.square(xf - mu), -1, keepdims=True)
    return ((xf - mu) * lax.rsqrt(var + LN_EPS) * g.astype(F32) + b.astype(F32)).astype(x.dtype)


def rms_norm(x, g):
    xf = x.astype(F32)
    y = xf * lax.rsqrt(jnp.mean(jnp.square(xf), -1, keepdims=True) + LN_EPS) * g.astype(F32)
    return y.astype(x.dtype)


def causal_depthwise_conv(x, w, b):
    c = x.shape[-1]
    y = lax.conv_general_dilated(
        x, w[:, None, :].astype(x.dtype), window_strides=(1,),
        padding=[(CONV_W - 1, 0)], dimension_numbers=('NWC', 'WIO', 'NWC'),
        feature_group_count=c)
    return y + b.astype(x.dtype)


def rg_lru(x, w_a, b_a, w_x, b_x, lam):
    bsz, s, _ = x.shape
    xb = x.reshape(bsz, s, RNN_BLOCKS, RNN_BLOCK)
    r = jax.nn.sigmoid(jnp.einsum('bsgi,gij->bsgj', xb, w_a).reshape(bsz, s, D_RNN).astype(F32) + b_a.astype(F32))
    i = jax.nn.sigmoid(jnp.einsum('bsgi,gij->bsgj', xb, w_x).reshape(bsz, s, D_RNN).astype(F32) + b_x.astype(F32))
    log_a = -RG_C * r * jax.nn.softplus(-lam.astype(F32))
    a = jnp.exp(log_a)
    u = jnp.sqrt(-jnp.expm1(2.0 * log_a)) * (i * x.astype(F32))

    def combine(left, right):
        a1, b1 = left
        a2, b2 = right
        return a1 * a2, a2 * b1 + b2

    _, h = lax.associative_scan(combine, (a, u), axis=1)
    return h.astype(x.dtype)


def t5_bucket(rel):
    half = N_BUCKETS // 2
    max_exact = half // 2
    ret = (rel > 0).astype(jnp.int32) * half
    n = jnp.abs(rel)
    large = max_exact + (jnp.log(jnp.maximum(n, 1).astype(F32) / max_exact)
                         / math.log(MAX_DIST / max_exact) * (half - max_exact)).astype(jnp.int32)
    large = jnp.minimum(large, half - 1)
    return ret + jnp.where(n < max_exact, n, large)


def relative_bias_blocks(rel_bias, s):
    pos = jnp.arange(s, dtype=jnp.int32)
    blocks = []
    for qb in range(s // Q_BLOCK):
        hi = (qb + 1) * Q_BLOCK
        qpos = pos[qb * Q_BLOCK:hi]
        kpos = pos[:hi]
        bucket = t5_bucket(kpos[None, :] - qpos[:, None])
        bias = jnp.transpose(rel_bias[bucket].astype(F32), (2, 0, 1))
        allowed = (kpos[None, :] // CHUNK) <= (qpos[:, None] // CHUNK)
        blocks.append(jnp.where(allowed[None], bias, NEG_INF))
    return blocks


def diff_attention(q, k, v, lq1, lk1, lq2, lk2, subln_g, lam_init, bias_blocks):
    bsz, s = q.shape[:2]
    qf = q.astype(F32) * (DIFF_QK ** -0.5)
    kf = k.astype(F32)
    vf = v.astype(F32)
    lam = (jnp.exp(jnp.sum(lq1.astype(F32) * lk1.astype(F32)))
           - jnp.exp(jnp.sum(lq2.astype(F32) * lk2.astype(F32))) + lam_init)
    outs = []
    for qb, bias in enumerate(bias_blocks):
        lo, hi = qb * Q_BLOCK, (qb + 1) * Q_BLOCK
        logits = jnp.einsum('bqhmd,bkhmd->bhmqk', qf[:, lo:hi], kf[:, :hi]) + bias[None, :, None]
        p = jax.nn.softmax(logits, axis=-1)
        w = p[:, :, 0] - lam * p[:, :, 1]
        outs.append(jnp.einsum('bhqk,bkhv->bqhv', w, vf[:, :hi]))
    o = jnp.concatenate(outs, axis=1)
    o = rms_norm(o, subln_g) * (1.0 - lam_init)
    return o.reshape(bsz, s, D_ATT).astype(v.dtype)


def hybrid_mixer(x, w_in, w_out, conv_w, conv_b, rg_wa, rg_ba, rg_wx, rg_bx, rg_lam, rnn_g,
                 lq1, lk1, lq2, lk2, subln_g, lam_init, bias_blocks):
    bsz, s, _ = x.shape
    proj = x @ w_in
    rnn_x, rnn_gate, q, k, v = jnp.split(
        proj, [D_RNN, 2 * D_RNN, 2 * D_RNN + D_ATT, 2 * D_RNN + 2 * D_ATT], axis=-1)
    h = rg_lru(causal_depthwise_conv(rnn_x, conv_w, conv_b), rg_wa, rg_ba, rg_wx, rg_bx, rg_lam)
    rnn_out = rms_norm(h, rnn_g) * jax.nn.gelu(rnn_gate, approximate=True)
    q = q.reshape(bsz, s, N_DIFF_HEADS, 2, DIFF_QK)
    k = k.reshape(bsz, s, N_DIFF_HEADS, 2, DIFF_QK)
    v = v.reshape(bsz, s, N_DIFF_HEADS, DIFF_VDIM)
    att_out = diff_attention(q, k, v, lq1, lk1, lq2, lk2, subln_g, lam_init, bias_blocks)
    return jnp.concatenate([rnn_out, att_out], axis=-1) @ w_out


def swiglu(x, wg, wu, wd):
    return (jax.nn.silu(x @ wg) * (x @ wu)) @ wd


def moe(x, w_router, b_router, w_e_gate, w_e_up, w_e_down, w_s_gate, w_s_up, w_s_down):
    bsz, s, d = x.shape
    xt = x.reshape(-1, d)
    t = xt.shape[0]
    scores = jax.nn.sigmoid((xt @ w_router).astype(F32))
    sel = scores + b_router.astype(F32)
    grp = sel.reshape(t, N_GROUPS, N_EXPERTS // N_GROUPS)
    grp_score = jnp.sum(lax.top_k(grp, 2)[0], axis=-1)
    _, top_grp = lax.top_k(grp_score, TOPK_GROUPS)
    grp_mask = jnp.sum(jax.nn.one_hot(top_grp, N_GROUPS, dtype=F32), axis=1) > 0
    exp_mask = jnp.repeat(grp_mask, N_EXPERTS // N_GROUPS, axis=1)
    _, top_e = lax.top_k(jnp.where(exp_mask, sel, -jnp.inf), TOP_K)
    g = jnp.take_along_axis(scores, top_e, axis=1)
    g = g / jnp.sum(g, axis=-1, keepdims=True) * ROUTED_SCALE
    a_n = t * TOP_K
    flat_e = top_e.reshape(a_n).astype(jnp.int32)
    flat_tok = jnp.repeat(jnp.arange(t, dtype=jnp.int32), TOP_K)
    flat_g = g.reshape(a_n)
    order = jnp.argsort(flat_e)
    se = flat_e[order]
    counts = jnp.bincount(flat_e, length=N_EXPERTS).astype(jnp.int32)
    padded = (counts + MOE_BLOCK - 1) // MOE_BLOCK * MOE_BLOCK
    start = jnp.cumsum(counts) - counts
    pend = jnp.cumsum(padded)
    pstart = pend - padded
    dest = pstart[se] + jnp.arange(a_n, dtype=jnp.int32) - start[se]
    n_blocks = -(-a_n // MOE_BLOCK) + N_EXPERTS
    rows = n_blocks * MOE_BLOCK
    row_tok = jnp.zeros((rows,), jnp.int32).at[dest].set(flat_tok[order])
    row_gate = jnp.zeros((rows,), F32).at[dest].set(flat_g[order])
    block_e = jnp.minimum(jnp.searchsorted(pend, jnp.arange(n_blocks, dtype=jnp.int32) * MOE_BLOCK,
                                           side='right'), N_EXPERTS - 1)

    def expert_block(args):
        tok, gate, e = args
        yb = swiglu(xt[tok], w_e_gate[e], w_e_up[e], w_e_down[e])
        return yb * gate[:, None].astype(yb.dtype)

    yr = lax.map(expert_block, (row_tok.reshape(n_blocks, MOE_BLOCK),
                                row_gate.reshape(n_blocks, MOE_BLOCK), block_e))
    routed = jax.ops.segment_sum(yr.reshape(rows, d), row_tok, num_segments=t)
    shared = swiglu(xt, w_s_gate, w_s_up, w_s_down)
    return (routed + shared).reshape(bsz, s, d).astype(x.dtype)


def setup_inputs(seed: int = 0) -> dict:
    key = jax.random.key(seed)
    ks = jax.random.split(key, 32)
    nrm = lambda k, shape, scale: jax.random.normal(k, shape, F32) * scale
    col_scale = jnp.concatenate([jnp.ones((W_IN_COLS - D_ATT,), F32), jnp.full((D_ATT,), BETA, F32)])
    a0 = jax.random.uniform(ks[10], (DEPTH, D_RNN), F32, minval=0.9, maxval=0.999)
    p0 = a0 ** (1.0 / RG_C)
    return {
        'x': nrm(ks[0], (BATCH, SEQ, D_MODEL), 1.0),
        'rel_bias': nrm(ks[1], (N_BUCKETS, N_DIFF_HEADS), 0.1),
        'w_in': nrm(ks[2], (DEPTH, D_MODEL, W_IN_COLS), D_MODEL ** -0.5) * col_scale,
        'w_out': nrm(ks[3], (DEPTH, D_MIX, D_MODEL), D_MIX ** -0.5 * BETA),
        'conv_w': nrm(ks[4], (DEPTH, CONV_W, D_RNN), CONV_W ** -0.5),
        'conv_b': nrm(ks[5], (DEPTH, D_RNN), 0.01),
        'rg_wa': nrm(ks[6], (DEPTH, RNN_BLOCKS, RNN_BLOCK, RNN_BLOCK), RNN_BLOCK ** -0.5),
        'rg_ba': nrm(ks[7], (DEPTH, D_RNN), 0.01),
        'rg_wx': nrm(ks[8], (DEPTH, RNN_BLOCKS, RNN_BLOCK, RNN_BLOCK), RNN_BLOCK ** -0.5),
        'rg_bx': nrm(ks[9], (DEPTH, D_RNN), 0.01),
        'rg_lambda': jnp.log(p0) - jnp.log1p(-p0),
        'rnn_norm_g': 1.0 + nrm(ks[11], (DEPTH, D_RNN), 0.02),
        'lambda_q1': nrm(ks[12], (DEPTH, DIFF_QK), 0.1),
        'lambda_k1': nrm(ks[13], (DEPTH, DIFF_QK), 0.1),
        'lambda_q2': nrm(ks[14], (DEPTH, DIFF_QK), 0.1),
        'lambda_k2': nrm(ks[15], (DEPTH, DIFF_QK), 0.1),
        'subln_g': 1.0 + nrm(ks[16], (DEPTH, DIFF_VDIM), 0.02),
        'ln1_g': 1.0 + nrm(ks[17], (DEPTH, D_MODEL), 0.02),
        'ln1_b': nrm(ks[18], (DEPTH, D_MODEL), 0.02),
        'w_router': nrm(ks[19], (DEPTH, D_MODEL, N_EXPERTS), D_MODEL ** -0.5),
        'b_router': nrm(ks[20], (DEPTH, N_EXPERTS), 0.01),
        'w_exp_gate': nrm(ks[21], (DEPTH, N_EXPERTS, D_MODEL, D_EXPERT), D_MODEL ** -0.5),
        'w_exp_up': nrm(ks[22], (DEPTH, N_EXPERTS, D_MODEL, D_EXPERT), D_MODEL ** -0.5),
        'w_exp_down': nrm(ks[23], (DEPTH, N_EXPERTS, D_EXPERT, D_MODEL), D_EXPERT ** -0.5 * BETA),
        'w_sh_gate': nrm(ks[24], (DEPTH, D_MODEL, D_SHARED), D_MODEL ** -0.5),
        'w_sh_up': nrm(ks[25], (DEPTH, D_MODEL, D_SHARED), D_MODEL ** -0.5),
        'w_sh_down': nrm(ks[26], (DEPTH, D_SHARED, D_MODEL), D_SHARED ** -0.5 * BETA),
        'ln2_g': 1.0 + nrm(ks[27], (DEPTH, D_MODEL), 0.02),
        'ln2_b': nrm(ks[28], (DEPTH, D_MODEL), 0.02),
    }


def reference(x, rel_bias, w_in, w_out, conv_w, conv_b, rg_wa, rg_ba, rg_wx, rg_bx, rg_lambda,
              rnn_norm_g, lambda_q1, lambda_k1, lambda_q2, lambda_k2, subln_g, ln1_g, ln1_b,
              w_router, b_router, w_exp_gate, w_exp_up, w_exp_down, w_sh_gate, w_sh_up,
              w_sh_down, ln2_g, ln2_b):
    s = x.shape[1]
    bias_blocks = relative_bias_blocks(rel_bias, s)
    for l in range(DEPTH):
        lam_init = 0.8 - 0.6 * math.exp(-0.3 * l)
        h = hybrid_mixer(x, w_in[l], w_out[l], conv_w[l], conv_b[l], rg_wa[l], rg_ba[l],
                         rg_wx[l], rg_bx[l], rg_lambda[l], rnn_norm_g[l], lambda_q1[l],
                         lambda_k1[l], lambda_q2[l], lambda_k2[l], subln_g[l], lam_init,
                         bias_blocks)
        x = layer_norm(ALPHA * x + h, ln1_g[l], ln1_b[l])
        h = moe(x, w_router[l], b_router[l], w_exp_gate[l], w_exp_up[l], w_exp_down[l],
                w_sh_gate[l], w_sh_up[l], w_sh_down[l])
        x = layer_norm(ALPHA * x + h, ln2_g[l], ln2_b[l])
    return x
```

```python
import functools
import math

import jax
import jax.numpy as jnp
from jax import lax
from jax.experimental import pallas as pl
from jax.experimental.pallas import tpu as pltpu

F32 = jnp.float32
BF16 = jnp.bfloat16

D_MODEL = 2048
MODEL_DEPTH = 4
CHUNK = 64
D_RNN = 1024
RNN_BLOCK = 64
CONV_W = 4
RG_C = 8.0
D_ATT = 1024
N_HEADS = 8
HEAD_V = 128
HEAD_QK = 64
N_BUCKETS = 32
MAX_DIST = 128
N_EXPERTS = 64
TOP_K = 8
N_GROUPS = 8
GROUP_SIZE = N_EXPERTS // N_GROUPS
TOPK_GROUPS = 4
D_EXPERT = 512
ROUTED_SCALE = 2.5
ALPHA = (2.0 * MODEL_DEPTH) ** 0.25
LN_EPS = 1e-5
NEG_INF = -1e30
W_IN_COLS = 2 * D_RNN + 3 * D_ATT
LANES = 128
SUBLANES = 8

VMEM_LIMIT = 56 * 1024 * 1024

ATT_TILE = 256
RNN_TILE = 256
MOE_BLOCK = 256
COMBINE_TILE = 128
ROUTER_TILE = 512


def _cparams(sem):
    return pltpu.CompilerParams(dimension_semantics=sem, vmem_limit_bytes=VMEM_LIMIT)


def _matmul_kernel(x_ref, w_ref, o_ref):
    o_ref[...] = jnp.dot(x_ref[...].astype(BF16), w_ref[...], preferred_element_type=F32)


def _in_proj(x, w_in_bf, layer, tm=512, tn=1024):
    m, k = x.shape
    n = w_in_bf.shape[2]
    return pl.pallas_call(
        _matmul_kernel,
        grid=(m // tm, n // tn),
        in_specs=[pl.BlockSpec((tm, k), lambda i, j: (i, 0)),
                  pl.BlockSpec((None, k, tn), lambda i, j: (layer, 0, j))],
        out_specs=pl.BlockSpec((tm, tn), lambda i, j: (i, j)),
        out_shape=jax.ShapeDtypeStruct((m, n), F32),
        compiler_params=_cparams(("parallel", "parallel")),
        name="in_proj",
    )(x, w_in_bf)


def _rglru_kernel(xr_ref, gate_ref, cw_ref, cb_ref, wa_ref, ba_ref, wx_ref, bx_ref, lam_ref, g_ref,
                  o_ref, prev_ref, h_ref, *, ts):
    @pl.when(pl.program_id(1) == 0)
    def _():
        prev_ref[...] = jnp.zeros_like(prev_ref)
        h_ref[...] = jnp.zeros_like(h_ref)

    x = xr_ref[...]
    xe = jnp.concatenate([prev_ref[...], x], axis=0)
    xc = x * cw_ref[CONV_W - 1:CONV_W, :] + cb_ref[...]
    for k in range(1, CONV_W):
        xc = xc + pltpu.roll(xe, k, axis=0)[SUBLANES:] * cw_ref[CONV_W - 1 - k:CONV_W - k, :]
    prev_ref[...] = x[ts - SUBLANES:]

    xcb = xc.astype(BF16)
    ga, gx = [], []
    for j in range(D_RNN // LANES):
        sl = xcb[:, LANES * j:LANES * (j + 1)]
        ga.append(jnp.dot(sl, wa_ref[j], preferred_element_type=F32))
        gx.append(jnp.dot(sl, wx_ref[j], preferred_element_type=F32))
    r = jax.nn.sigmoid(jnp.concatenate(ga, axis=1) + ba_ref[...])
    gi = jax.nn.sigmoid(jnp.concatenate(gx, axis=1) + bx_ref[...])
    z = -lam_ref[...]
    softplus = jnp.maximum(z, 0.0) + jnp.log1p(jnp.exp(-jnp.abs(z)))
    log_a = (-RG_C) * r * softplus
    a = jnp.exp(log_a)
    u = jnp.sqrt(-jnp.tanh(log_a) * (a * a + 1.0)) * (gi * xc)

    row = lax.broadcasted_iota(jnp.int32, (ts, D_RNN), 0)
    k = 1
    while k < SUBLANES:
        keep = row >= k
        a_sh = jnp.where(keep, pltpu.roll(a, k, axis=0), 1.0)
        u_sh = jnp.where(keep, pltpu.roll(u, k, axis=0), 0.0)
        u = a * u_sh + u
        a = a * a_sh
        k *= 2
    while k < ts:
        u = jnp.concatenate([u[:k], a[k:] * u[:ts - k] + u[k:]], axis=0)
        a = jnp.concatenate([a[:k], a[k:] * a[:ts - k]], axis=0)
        k *= 2
    h = a * h_ref[...] + u
    h_ref[...] = h[ts - 1:ts]

    ms = jnp.mean(h * h, axis=-1, keepdims=True)
    y = h * lax.rsqrt(ms + LN_EPS) * g_ref[...]
    o_ref[...] = y * jax.nn.gelu(gate_ref[...], approximate=True)


def _rglru(proj, bsz, seq, layer, conv_w, conv_b, wa_bd, rg_ba, wx_bd, rg_bx, rg_lam, rnn_g):
    ts = min(RNN_TILE, seq)
    ns = seq // ts
    row = lambda shape: pl.BlockSpec((None,) + shape, lambda b, s: (layer,) + (0,) * len(shape))
    return pl.pallas_call(
        functools.partial(_rglru_kernel, ts=ts),
        grid=(bsz, ns),
        in_specs=[pl.BlockSpec((ts, D_RNN), lambda b, s: (b * ns + s, 0)),
                  pl.BlockSpec((ts, D_RNN), lambda b, s: (b * ns + s, 1)),
                  row((CONV_W, D_RNN)), row((1, D_RNN)),
                  row((D_RNN // LANES, LANES, LANES)), row((1, D_RNN)),
                  row((D_RNN // LANES, LANES, LANES)), row((1, D_RNN)),
                  row((1, D_RNN)), row((1, D_RNN))],
        out_specs=pl.BlockSpec((ts, D_RNN), lambda b, s: (b * ns + s, 0)),
        out_shape=jax.ShapeDtypeStruct((bsz * seq, D_RNN), F32),
        scratch_shapes=[pltpu.VMEM((SUBLANES, D_RNN), F32), pltpu.VMEM((1, D_RNN), F32)],
        compiler_params=_cparams(("parallel", "arbitrary")),
        name="rglru",
    )(proj, proj, conv_w, conv_b, wa_bd, rg_ba, wx_bd, rg_bx, rg_lam, rnn_g)


def _attn_kernel(far_ref, q_ref, k_ref, v_ref, bias_ref, lam_ref, g_ref, o_ref, m_sc, l_sc, acc_sc,
                 *, tile, lam_init):
    head = pl.program_id(1)
    qi = pl.program_id(2)
    lane = lax.broadcasted_iota(jnp.int32, (tile, HEAD_V), 1)
    q = q_ref[...] * (HEAD_QK ** -0.5)
    qm = [jnp.where(lane < HEAD_QK, q, 0.0).astype(BF16), jnp.where(lane >= HEAD_QK, q, 0.0).astype(BF16)]
    m_sc[...] = jnp.full_like(m_sc, NEG_INF)
    l_sc[...] = jnp.zeros_like(l_sc)
    acc_sc[...] = jnp.zeros_like(acc_sc)

    def block(j, bias):
        start = pl.multiple_of(j * tile, tile)
        kb = k_ref[pl.ds(start, tile), :].astype(BF16)
        vb = v_ref[pl.ds(start, tile), :].astype(BF16)
        for m in range(2):
            s = lax.dot_general(qm[m], kb, (((1,), (1,)), ((), ())), preferred_element_type=F32) + bias
            m_old = m_sc[m]
            m_new = jnp.maximum(m_old, jnp.max(s, axis=1, keepdims=True))
            scale = jnp.exp(m_old - m_new)
            p = jnp.exp(s - m_new)
            l_sc[m] = scale * l_sc[m] + jnp.sum(p, axis=1, keepdims=True)
            acc_sc[m] = scale * acc_sc[m] + jnp.dot(p.astype(BF16), vb, preferred_element_type=F32)
            m_sc[m] = m_new

    far_bias = far_ref[head]

    def far_body(j, carry):
        block(j, far_bias)
        return carry

    lax.fori_loop(0, jnp.maximum(qi - 1, 0), far_body, 0)

    @pl.when(qi >= 1)
    def _():
        block(qi - 1, bias_ref[1])

    block(qi, bias_ref[0])

    lam4 = lam_ref[...]
    lam = (jnp.exp(jnp.sum(lam4[0:1] * lam4[1:2], axis=1, keepdims=True))
           - jnp.exp(jnp.sum(lam4[2:3] * lam4[3:4], axis=1, keepdims=True)) + lam_init)
    o = acc_sc[0] / l_sc[0] - lam * (acc_sc[1] / l_sc[1])
    ms = jnp.mean(o * o, axis=-1, keepdims=True)
    o_ref[...] = o * lax.rsqrt(ms + LN_EPS) * g_ref[...] * (1.0 - lam_init)


def _attention(proj, bsz, seq, layer, far_bias, bias_tiles, lam4, subln_g, lam_init):
    tile = min(ATT_TILE, seq)
    nq = seq // tile
    qcol = 2 * D_RNN // HEAD_V
    kcol = qcol + D_ATT // HEAD_V
    vcol = kcol + D_ATT // HEAD_V
    grid_spec = pltpu.PrefetchScalarGridSpec(
        num_scalar_prefetch=1,
        grid=(bsz, N_HEADS, nq),
        in_specs=[pl.BlockSpec((tile, HEAD_V), lambda b, h, i, far: (b * nq + i, qcol + h)),
                  pl.BlockSpec((seq, HEAD_V), lambda b, h, i, far: (b, kcol + h)),
                  pl.BlockSpec((seq, HEAD_V), lambda b, h, i, far: (b, vcol + h)),
                  pl.BlockSpec((None, 2, tile, tile), lambda b, h, i, far: (h, 0, 0, 0)),
                  pl.BlockSpec((None, 4, HEAD_QK), lambda b, h, i, far: (layer, 0, 0)),
                  pl.BlockSpec((None, 1, HEAD_V), lambda b, h, i, far: (layer, 0, 0))],
        out_specs=pl.BlockSpec((tile, HEAD_V), lambda b, h, i, far: (b * nq + i, h)),
        scratch_shapes=[pltpu.VMEM((2, tile, 1), F32), pltpu.VMEM((2, tile, 1), F32),
                        pltpu.VMEM((2, tile, HEAD_V), F32)])
    return pl.pallas_call(
        functools.partial(_attn_kernel, tile=tile, lam_init=lam_init),
        grid_spec=grid_spec,
        out_shape=jax.ShapeDtypeStruct((bsz * seq, D_ATT), F32),
        compiler_params=_cparams(("parallel", "parallel", "arbitrary")),
        name="diff_attn",
    )(far_bias, proj, proj, proj, bias_tiles, lam4, subln_g)


def _t5_bucket(rel):
    half = N_BUCKETS // 2
    max_exact = half // 2
    ret = (rel > 0).astype(jnp.int32) * half
    n = jnp.abs(rel)
    large = max_exact + (jnp.log(jnp.maximum(n, 1).astype(F32) / max_exact)
                         / math.log(MAX_DIST / max_exact) * (half - max_exact)).astype(jnp.int32)
    large = jnp.minimum(large, half - 1)
    return ret + jnp.where(n < max_exact, n, large)


def _bias_tables(rel_bias, tile):
    assert tile >= MAX_DIST and tile % CHUNK == 0
    qpos = jnp.arange(tile, dtype=jnp.int32)[:, None]
    kpos = jnp.arange(tile, dtype=jnp.int32)[None, :]
    diag = rel_bias[_t5_bucket(kpos - qpos)].astype(F32)
    diag = jnp.where(((kpos // CHUNK) <= (qpos // CHUNK))[..., None], diag, NEG_INF)
    prev = rel_bias[_t5_bucket(kpos - tile - qpos)].astype(F32)
    tiles = jnp.transpose(jnp.stack([diag, prev], axis=0), (3, 0, 1, 2))
    far = rel_bias[_t5_bucket(jnp.int32(-tile - 1))].astype(F32)
    return tiles, far


def _layer_norm(y, g, b):
    mu = jnp.mean(y, axis=-1, keepdims=True)
    d = y - mu
    var = jnp.mean(d * d, axis=-1, keepdims=True)
    return d * lax.rsqrt(var + LN_EPS) * g + b


def _out_proj_kernel(x_ref, r_ref, a_ref, w_ref, g_ref, b_ref, o_ref):
    h = jnp.dot(r_ref[...].astype(BF16), w_ref[0:D_RNN, :], preferred_element_type=F32)
    h = h + jnp.dot(a_ref[...].astype(BF16), w_ref[D_RNN:, :], preferred_element_type=F32)
    o_ref[...] = _layer_norm(ALPHA * x_ref[...] + h, g_ref[...], b_ref[...])


def _out_proj_ln(x, rnn_out, att_out, w_out_bf, ln_g, ln_b, layer, tm=256):
    t = x.shape[0]
    vec = pl.BlockSpec((None, 1, D_MODEL), lambda i: (layer, 0, 0))
    return pl.pallas_call(
        _out_proj_kernel,
        grid=(t // tm,),
        in_specs=[pl.BlockSpec((tm, D_MODEL), lambda i: (i, 0)),
                  pl.BlockSpec((tm, D_RNN), lambda i: (i, 0)),
                  pl.BlockSpec((tm, D_ATT), lambda i: (i, 0)),
                  pl.BlockSpec((None, D_MODEL, D_MODEL), lambda i: (layer, 0, 0)),
                  vec, vec],
        out_specs=pl.BlockSpec((tm, D_MODEL), lambda i: (i, 0)),
        out_shape=jax.ShapeDtypeStruct((t, D_MODEL), F32),
        compiler_params=_cparams(("parallel",)),
        name="out_proj_ln",
    )(x, rnn_out, att_out, w_out_bf, ln_g, ln_b)


def _first_index(hit, idx, size):
    return jnp.min(jnp.where(hit, idx, float(size)), axis=0, keepdims=True)


def _router_kernel(x_ref, w_ref, b_ref, e_ref, g_ref, p_ref, c_ref, cnt_sc, *, tm):
    @pl.when(pl.program_id(0) == 0)
    def _():
        cnt_sc[...] = jnp.zeros_like(cnt_sc)

    logits = lax.dot_general(w_ref[...], x_ref[...], (((1,), (1,)), ((), ())),
                             precision=lax.Precision.HIGHEST, preferred_element_type=F32)
    scores = jax.nn.sigmoid(logits)
    sel = scores + b_ref[:, 0:1]

    grp = sel.reshape(N_GROUPS, GROUP_SIZE, tm)
    gidx = lax.broadcasted_iota(jnp.int32, grp.shape, 1).astype(F32)
    m1 = jnp.max(grp, axis=1, keepdims=True)
    first = jnp.min(jnp.where(grp == m1, gidx, float(GROUP_SIZE)), axis=1, keepdims=True)
    m2 = jnp.max(jnp.where(gidx == first, -jnp.inf, grp), axis=1, keepdims=True)
    gscore = (m1 + m2).reshape(N_GROUPS, tm)

    ridx = lax.broadcasted_iota(jnp.int32, (N_GROUPS, tm), 0).astype(F32)
    gmask = jnp.zeros((N_GROUPS, tm), F32)
    for _ in range(TOPK_GROUPS):
        best = jnp.max(gscore, axis=0, keepdims=True)
        pick = ridx == _first_index(gscore == best, ridx, N_GROUPS)
        gmask = jnp.where(pick, 1.0, gmask)
        gscore = jnp.where(pick, -jnp.inf, gscore)
    emask = jnp.broadcast_to(gmask.reshape(N_GROUPS, 1, tm), (N_GROUPS, GROUP_SIZE, tm)).reshape(N_EXPERTS, tm)
    masked = jnp.where(emask > 0.0, sel, -jnp.inf)

    eidx = lax.broadcasted_iota(jnp.int32, (N_EXPERTS, tm), 0).astype(F32)
    chosen = jnp.zeros((N_EXPERTS, tm), F32)
    picks, gates = [], []
    for _ in range(TOP_K):
        best = jnp.max(masked, axis=0, keepdims=True)
        e_k = _first_index(masked == best, eidx, N_EXPERTS)
        pick = eidx == e_k
        picks.append(e_k)
        gates.append(jnp.sum(jnp.where(pick, scores, 0.0), axis=0, keepdims=True))
        chosen = jnp.where(pick, 1.0, chosen)
        masked = jnp.where(pick, -jnp.inf, masked)
    gate = jnp.concatenate(gates, axis=0)
    g_ref[...] = gate / jnp.sum(gate, axis=0, keepdims=True) * ROUTED_SCALE
    e_ref[...] = jnp.concatenate(picks, axis=0).astype(jnp.int32)

    before = (lax.broadcasted_iota(jnp.int32, (tm, tm), 0) < lax.broadcasted_iota(jnp.int32, (tm, tm), 1))
    rank = jnp.dot(chosen.astype(BF16), before.astype(BF16), preferred_element_type=F32) + cnt_sc[:, 0:1]
    p_ref[...] = jnp.concatenate(
        [jnp.sum(jnp.where(eidx == e_k, rank, 0.0), axis=0, keepdims=True) for e_k in picks],
        axis=0).astype(jnp.int32)
    cnt_sc[...] = cnt_sc[...] + jnp.sum(chosen, axis=1, keepdims=True)
    c_ref[...] = cnt_sc[...]


def _router(x, w_router_t, b_router_col, layer):
    t = x.shape[0]
    tm = min(ROUTER_TILE, t)
    return pl.pallas_call(
        functools.partial(_router_kernel, tm=tm),
        grid=(t // tm,),
        in_specs=[pl.BlockSpec((tm, D_MODEL), lambda i: (i, 0)),
                  pl.BlockSpec((None, N_EXPERTS, D_MODEL), lambda i: (layer, 0, 0)),
                  pl.BlockSpec((None, N_EXPERTS, LANES), lambda i: (layer, 0, 0))],
        out_specs=[pl.BlockSpec((TOP_K, tm), lambda i: (0, i)),
                   pl.BlockSpec((TOP_K, tm), lambda i: (0, i)),
                   pl.BlockSpec((TOP_K, tm), lambda i: (0, i)),
                   pl.BlockSpec((N_EXPERTS, LANES), lambda i: (0, 0))],
        out_shape=[jax.ShapeDtypeStruct((TOP_K, t), jnp.int32),
                   jax.ShapeDtypeStruct((TOP_K, t), F32),
                   jax.ShapeDtypeStruct((TOP_K, t), jnp.int32),
                   jax.ShapeDtypeStruct((N_EXPERTS, LANES), F32)],
        scratch_shapes=[pltpu.VMEM((N_EXPERTS, LANES), F32)],
        compiler_params=_cparams(("arbitrary",)),
        name="router",
    )(x, w_router_t, b_router_col)


def _row_gather(idx_ref, n, src_hbm, dst, sem):
    def body(r, carry):
        pltpu.make_async_copy(src_hbm.at[pl.ds(idx_ref[0, 0, r], 1)], dst.at[pl.ds(r, 1)], sem).start()
        return carry
    lax.fori_loop(0, n, body, 0, unroll=8)


def _expert_kernel(be_ref, nu_ref, tokc_ref, tokn_ref, x_hbm, wg_ref, wu_ref, wd_ref, o_ref, xbuf, sem, *, blk):
    i = pl.program_id(0)
    n_used = nu_ref[0]
    slot = i % 2

    @pl.when(i == 0)
    def _():
        _row_gather(tokc_ref, blk, x_hbm, xbuf.at[0], sem.at[0])

    @pl.when(i + 1 < n_used)
    def _():
        _row_gather(tokn_ref, blk, x_hbm, xbuf.at[1 - slot], sem.at[1 - slot])

    @pl.when(i < n_used)
    def _():
        pltpu.make_async_copy(xbuf.at[slot], xbuf.at[slot], sem.at[slot]).wait()
        xb = xbuf[slot].astype(BF16)
        gate = jnp.dot(xb, wg_ref[...], preferred_element_type=F32)
        up = jnp.dot(xb, wu_ref[...], preferred_element_type=F32)
        hidden = (jax.nn.silu(gate) * up).astype(BF16)
        o_ref[...] = jnp.dot(hidden, wd_ref[...], preferred_element_type=F32)

    @pl.when(i >= n_used)
    def _():
        o_ref[...] = jnp.zeros_like(o_ref)


def _experts(x, row_tok, block_e, n_used, wg_bf, wu_bf, wd_bf, layer, blk):
    n_blocks = row_tok.shape[0]
    last = n_blocks - 1
    grid_spec = pltpu.PrefetchScalarGridSpec(
        num_scalar_prefetch=2,
        grid=(n_blocks,),
        in_specs=[pl.BlockSpec((1, 1, blk), lambda i, be, nu: (i, 0, 0), memory_space=pltpu.SMEM),
                  pl.BlockSpec((1, 1, blk), lambda i, be, nu: (jnp.minimum(i + 1, last), 0, 0),
                               memory_space=pltpu.SMEM),
                  pl.BlockSpec(memory_space=pl.ANY),
                  pl.BlockSpec((None, None, D_MODEL, D_EXPERT), lambda i, be, nu: (layer, be[i], 0, 0)),
                  pl.BlockSpec((None, None, D_MODEL, D_EXPERT), lambda i, be, nu: (layer, be[i], 0, 0)),
                  pl.BlockSpec((None, None, D_EXPERT, D_MODEL), lambda i, be, nu: (layer, be[i], 0, 0))],
        out_specs=pl.BlockSpec((blk, D_MODEL), lambda i, be, nu: (i, 0)),
        scratch_shapes=[pltpu.VMEM((2, blk, D_MODEL), F32), pltpu.SemaphoreType.DMA((2,))])
    return pl.pallas_call(
        functools.partial(_expert_kernel, blk=blk),
        grid_spec=grid_spec,
        out_shape=jax.ShapeDtypeStruct((n_blocks * blk, D_MODEL), F32),
        compiler_params=_cparams(("arbitrary",)),
        name="experts",
    )(block_e, n_used, row_tok, row_tok, x, wg_bf, wu_bf, wd_bf)


def _combine_kernel(dc_ref, dn_ref, x_ref, gate_ref, y_hbm, wg_ref, wu_ref, wd_ref, g_ref, b_ref, o_ref,
                    ybuf, sem, *, tm, n_tiles):
    i = pl.program_id(0)
    slot = i % 2

    def gather(idx_ref, s):
        _row_gather(idx_ref, TOP_K * tm, y_hbm, ybuf.at[s], sem.at[s])

    @pl.when(i == 0)
    def _():
        gather(dc_ref, 0)

    @pl.when(i + 1 < n_tiles)
    def _():
        gather(dn_ref, 1 - slot)

    x = x_ref[...]
    xb = x.astype(BF16)
    hidden = (jax.nn.silu(jnp.dot(xb, wg_ref[...], preferred_element_type=F32))
              * jnp.dot(xb, wu_ref[...], preferred_element_type=F32)).astype(BF16)
    y = ALPHA * x + jnp.dot(hidden, wd_ref[...], preferred_element_type=F32)

    pltpu.make_async_copy(ybuf.at[slot], ybuf.at[slot], sem.at[slot]).wait()
    gate = gate_ref[...]
    for k in range(TOP_K):
        y = y + gate[:, k:k + 1] * ybuf[slot, pl.ds(k * tm, tm), :]
    o_ref[...] = _layer_norm(y, g_ref[...], b_ref[...])


def _combine(x, gate_tk, dest_tiles, y_rows, wsg_bf, wsu_bf, wsd_bf, ln_g, ln_b, layer, tm):
    t = x.shape[0]
    n_tiles = t // tm
    last = n_tiles - 1
    vec = pl.BlockSpec((None, 1, D_MODEL), lambda i: (layer, 0, 0))
    return pl.pallas_call(
        functools.partial(_combine_kernel, tm=tm, n_tiles=n_tiles),
        grid=(n_tiles,),
        in_specs=[pl.BlockSpec((1, 1, TOP_K * tm), lambda i: (i, 0, 0), memory_space=pltpu.SMEM),
                  pl.BlockSpec((1, 1, TOP_K * tm), lambda i: (jnp.minimum(i + 1, last), 0, 0),
                               memory_space=pltpu.SMEM),
                  pl.BlockSpec((tm, D_MODEL), lambda i: (i, 0)),
                  pl.BlockSpec((tm, TOP_K), lambda i: (i, 0)),
                  pl.BlockSpec(memory_space=pl.ANY),
                  pl.BlockSpec((None, D_MODEL, D_EXPERT), lambda i: (layer, 0, 0)),
                  pl.BlockSpec((None, D_MODEL, D_EXPERT), lambda i: (layer, 0, 0)),
                  pl.BlockSpec((None, D_EXPERT, D_MODEL), lambda i: (layer, 0, 0)),
                  vec, vec],
        out_specs=pl.BlockSpec((tm, D_MODEL), lambda i: (i, 0)),
        out_shape=jax.ShapeDtypeStruct((t, D_MODEL), F32),
        scratch_shapes=[pltpu.VMEM((2, TOP_K * tm, D_MODEL), F32), pltpu.SemaphoreType.DMA((2,))],
        compiler_params=_cparams(("arbitrary",)),
        name="combine_ln",
    )(dest_tiles, dest_tiles, x, gate_tk, y_rows, wsg_bf, wsu_bf, wsd_bf, ln_g, ln_b)


def _dispatch_tables(top_e, rank, counts, blk, tm):
    k, t = top_e.shape
    n_blocks = (k * t) // blk + N_EXPERTS
    counts = counts[:, 0].astype(jnp.int32)
    padded = (counts + blk - 1) // blk * blk
    pend = jnp.cumsum(padded)
    pstart = pend - padded
    dest = pstart[top_e] + rank
    tok = jnp.broadcast_to(jnp.arange(t, dtype=jnp.int32)[None, :], (k, t))
    row_tok = jnp.zeros((n_blocks * blk,), jnp.int32).at[dest.reshape(-1)].set(tok.reshape(-1))
    n_used = (pend[-1] // blk).astype(jnp.int32)
    blocks = jnp.arange(n_blocks, dtype=jnp.int32)
    block_e = jnp.minimum(jnp.searchsorted(pend, jnp.minimum(blocks, n_used - 1) * blk, side='right'),
                          N_EXPERTS - 1).astype(jnp.int32)
    dest_tiles = dest.reshape(k, t // tm, tm).transpose(1, 0, 2).reshape(t // tm, 1, k * tm)
    return row_tok.reshape(n_blocks, 1, blk), block_e, n_used.reshape(1), dest_tiles


def kernel(x, rel_bias, w_in, w_out, conv_w, conv_b, rg_wa, rg_ba, rg_wx, rg_bx, rg_lambda, rnn_norm_g,
           lambda_q1, lambda_k1, lambda_q2, lambda_k2, subln_g, ln1_g, ln1_b, w_router, b_router,
           w_exp_gate, w_exp_up, w_exp_down, w_sh_gate, w_sh_up, w_sh_down, ln2_g, ln2_b):
    bsz, seq, d = x.shape
    depth = w_in.shape[0]
    t = bsz * seq
    blk = min(MOE_BLOCK, t)
    ctile = min(COMBINE_TILE, t)

    w_in_bf = w_in.astype(BF16)
    w_out_bf = w_out.astype(BF16)
    wg_bf, wu_bf, wd_bf = w_exp_gate.astype(BF16), w_exp_up.astype(BF16), w_exp_down.astype(BF16)
    wsg_bf, wsu_bf, wsd_bf = w_sh_gate.astype(BF16), w_sh_up.astype(BF16), w_sh_down.astype(BF16)
    w_router_t = jnp.swapaxes(w_router, 1, 2)
    b_router_col = jnp.broadcast_to(b_router[:, :, None], (depth, N_EXPERTS, LANES))

    def pair_blocks(w):
        w = w.reshape(depth, D_RNN // LANES, 2, RNN_BLOCK, RNN_BLOCK)
        z = jnp.zeros_like(w[:, :, 0])
        top = jnp.concatenate([w[:, :, 0], z], axis=-1)
        bot = jnp.concatenate([z, w[:, :, 1]], axis=-1)
        return jnp.concatenate([top, bot], axis=-2).astype(BF16)

    wa_bd, wx_bd = pair_blocks(rg_wa), pair_blocks(rg_wx)
    row3 = lambda a: a[:, None, :]
    lam4 = jnp.stack([lambda_q1, lambda_k1, lambda_q2, lambda_k2], axis=1)
    bias_tiles, far_bias = _bias_tables(rel_bias, min(ATT_TILE, seq))

    xt = x.reshape(t, d)
    for l in range(depth):
        lam_init = 0.8 - 0.6 * math.exp(-0.3 * l)
        proj = _in_proj(xt, w_in_bf, l)
        rnn_out = _rglru(proj, bsz, seq, l, conv_w, row3(conv_b), wa_bd, row3(rg_ba), wx_bd, row3(rg_bx),
                         row3(rg_lambda), row3(rnn_norm_g))
        att_out = _attention(proj, bsz, seq, l, far_bias, bias_tiles, lam4, row3(subln_g), lam_init)
        xt = _out_proj_ln(xt, rnn_out, att_out, w_out_bf, row3(ln1_g), row3(ln1_b), l)
        top_e, gate, rank, counts = _router(xt, w_router_t, b_router_col, l)
        row_tok, block_e, n_used, dest_tiles = _dispatch_tables(top_e, rank, counts, blk, ctile)
        y_rows = _experts(xt, row_tok, block_e, n_used, wg_bf, wu_bf, wd_bf, l, blk)
        xt = _combine(xt, gate.T, dest_tiles, y_rows, wsg_bf, wsu_bf, wsd_bf, row3(ln2_g), row3(ln2_b), l, ctile)
    return xt.reshape(bsz, seq, d)
```

```python
import functools
import math

import jax
import jax.numpy as jnp
from jax import lax
from jax.experimental import pallas as pl
from jax.experimental.pallas import tpu as pltpu

F32 = jnp.float32
BF16 = jnp.bfloat16

D_MODEL = 2048
MODEL_DEPTH = 4
CHUNK = 64
D_RNN = 1024
RNN_BLOCK = 64
CONV_W = 4
RG_C = 8.0
D_ATT = 1024
N_HEADS = 8
HEAD_V = 128
HEAD_QK = 64
N_BUCKETS = 32
MAX_DIST = 128
N_EXPERTS = 64
TOP_K = 8
N_GROUPS = 8
GROUP_SIZE = N_EXPERTS // N_GROUPS
TOPK_GROUPS = 4
D_EXPERT = 512
ROUTED_SCALE = 2.5
ALPHA = (2.0 * MODEL_DEPTH) ** 0.25
LN_EPS = 1e-5
NEG_INF = -1e30
W_IN_COLS = 2 * D_RNN + 3 * D_ATT
LANES = 128
SUBLANES = 8

VMEM_LIMIT = 56 * 1024 * 1024

ATT_TILE = 256
RNN_TILE = 256
MOE_BLOCK = 256
COMBINE_TILE = 128
ROUTER_TILE = 512


def _cparams(sem):
    return pltpu.CompilerParams(dimension_semantics=sem, vmem_limit_bytes=VMEM_LIMIT)


def _matmul_kernel(x_ref, w_ref, o_ref):
    o_ref[...] = jnp.dot(x_ref[...].astype(BF16), w_ref[...], preferred_element_type=F32)


def _in_proj(x, w_in_bf, layer, tm=512, tn=1024):
    m, k = x.shape
    n = w_in_bf.shape[2]
    return pl.pallas_call(
        _matmul_kernel,
        grid=(m // tm, n // tn),
        in_specs=[pl.BlockSpec((tm, k), lambda i, j: (i, 0)),
                  pl.BlockSpec((None, k, tn), lambda i, j: (layer, 0, j))],
        out_specs=pl.BlockSpec((tm, tn), lambda i, j: (i, j)),
        out_shape=jax.ShapeDtypeStruct((m, n), F32),
        compiler_params=_cparams(("parallel", "parallel")),
        name="in_proj",
    )(x, w_in_bf)


def _rglru_kernel(xr_ref, gate_ref, cw_ref, cb_ref, wa_ref, ba_ref, wx_ref, bx_ref, lam_ref, g_ref,
                  o_ref, prev_ref, h_ref, *, ts):
    @pl.when(pl.program_id(1) == 0)
    def _():
        prev_ref[...] = jnp.zeros_like(prev_ref)
        h_ref[...] = jnp.zeros_like(h_ref)

    x = xr_ref[...]
    xe = jnp.concatenate([prev_ref[...], x], axis=0)
    xc = x * cw_ref[CONV_W - 1:CONV_W, :] + cb_ref[...]
    for k in range(1, CONV_W):
        xc = xc + pltpu.roll(xe, k, axis=0)[SUBLANES:] * cw_ref[CONV_W - 1 - k:CONV_W - k, :]
    prev_ref[...] = x[ts - SUBLANES:]

    xcb = xc.astype(BF16)
    ga, gx = [], []
    for j in range(D_RNN // LANES):
        sl = xcb[:, LANES * j:LANES * (j + 1)]
        ga.append(jnp.dot(sl, wa_ref[j], preferred_element_type=F32))
        gx.append(jnp.dot(sl, wx_ref[j], preferred_element_type=F32))
    r = jax.nn.sigmoid(jnp.concatenate(ga, axis=1) + ba_ref[...])
    gi = jax.nn.sigmoid(jnp.concatenate(gx, axis=1) + bx_ref[...])
    z = -lam_ref[...]
    softplus = jnp.maximum(z, 0.0) + jnp.log1p(jnp.exp(-jnp.abs(z)))
    log_a = (-RG_C) * r * softplus
    a = jnp.exp(log_a)
    u = jnp.sqrt(-jnp.tanh(log_a) * (a * a + 1.0)) * (gi * xc)

    row = lax.broadcasted_iota(jnp.int32, (ts, D_RNN), 0)
    k = 1
    while k < SUBLANES:
        keep = row >= k
        a_sh = jnp.where(keep, pltpu.roll(a, k, axis=0), 1.0)
        u_sh = jnp.where(keep, pltpu.roll(u, k, axis=0), 0.0)
        u = a * u_sh + u
        a = a * a_sh
        k *= 2
    while k < ts:
        u = jnp.concatenate([u[:k], a[k:] * u[:ts - k] + u[k:]], axis=0)
        a = jnp.concatenate([a[:k], a[k:] * a[:ts - k]], axis=0)
        k *= 2
    h = a * h_ref[...] + u
    h_ref[...] = h[ts - 1:ts]

    ms = jnp.mean(h * h, axis=-1, keepdims=True)
    y = h * lax.rsqrt(ms + LN_EPS) * g_ref[...]
    o_ref[...] = y * jax.nn.gelu(gate_ref[...], approximate=True)


def _rglru(proj, bsz, seq, layer, conv_w, conv_b, wa_bd, rg_ba, wx_bd, rg_bx, rg_lam, rnn_g):
    ts = min(RNN_TILE, seq)
    ns = seq // ts
    row = lambda shape: pl.BlockSpec((None,) + shape, lambda b, s: (layer,) + (0,) * len(shape))
    return pl.pallas_call(
        functools.partial(_rglru_kernel, ts=ts),
        grid=(bsz, ns),
        in_specs=[pl.BlockSpec((ts, D_RNN), lambda b, s: (b * ns + s, 0)),
                  pl.BlockSpec((ts, D_RNN), lambda b, s: (b * ns + s, 1)),
                  row((CONV_W, D_RNN)), row((1, D_RNN)),
                  row((D_RNN // LANES, LANES, LANES)), row((1, D_RNN)),
                  row((D_RNN // LANES, LANES, LANES)), row((1, D_RNN)),
                  row((1, D_RNN)), row((1, D_RNN))],
        out_specs=pl.BlockSpec((ts, D_RNN), lambda b, s: (b * ns + s, 0)),
        out_shape=jax.ShapeDtypeStruct((bsz * seq, D_RNN), F32),
        scratch_shapes=[pltpu.VMEM((SUBLANES, D_RNN), F32), pltpu.VMEM((1, D_RNN), F32)],
        compiler_params=_cparams(("parallel", "arbitrary")),
        name="rglru",
    )(proj, proj, conv_w, conv_b, wa_bd, rg_ba, wx_bd, rg_bx, rg_lam, rnn_g)


def _attn_kernel(far_ref, q_ref, k_ref, v_ref, bias_ref, lam_ref, g_ref, o_ref,
                 kbf_sc, vt_sc, m_sc, l_sc, acc_sc, *, tile, n_kv, lam_init):
    head = pl.program_id(1)
    qi = pl.program_id(2)

    @pl.when(qi == 0)
    def _():
        def prep(j, carry):
            start = pl.multiple_of(j * tile, tile)
            kbf_sc[j] = k_ref[pl.ds(start, tile), :].astype(BF16)
            vt_sc[j] = v_ref[pl.ds(start, tile), :].T.astype(BF16)
            return carry
        lax.fori_loop(0, n_kv, prep, 0)

    q_t = (q_ref[...] * (HEAD_QK ** -0.5)).T
    row = lax.broadcasted_iota(jnp.int32, (HEAD_V, tile), 0)
    q_blk = jnp.concatenate([jnp.where(row < HEAD_QK, q_t, 0.0), jnp.where(row >= HEAD_QK, q_t, 0.0)],
                            axis=1).astype(BF16)
    m_sc[...] = jnp.full_like(m_sc, NEG_INF)
    l_sc[...] = jnp.zeros_like(l_sc)
    acc_sc[...] = jnp.zeros_like(acc_sc)

    def block(j, bias):
        s = jnp.dot(kbf_sc[j], q_blk, preferred_element_type=F32) + bias
        m_old = m_sc[...]
        m_new = jnp.maximum(m_old, jnp.max(s, axis=0, keepdims=True))
        scale = jnp.exp(m_old - m_new)
        p = jnp.exp(s - m_new)
        l_sc[...] = scale * l_sc[...] + jnp.sum(p, axis=0, keepdims=True)
        acc_sc[...] = scale * acc_sc[...] + jnp.dot(vt_sc[j], p.astype(BF16), preferred_element_type=F32)
        m_sc[...] = m_new

    far_bias = far_ref[head]

    def far_body(j, carry):
        block(j, far_bias)
        return carry

    lax.fori_loop(0, jnp.maximum(qi - 1, 0), far_body, 0)

    @pl.when(qi >= 1)
    def _():
        block(qi - 1, bias_ref[1])

    block(qi, bias_ref[0])

    lam4 = lam_ref[...]
    lam = (jnp.exp(jnp.sum(lam4[0:1] * lam4[1:2], axis=1, keepdims=True))
           - jnp.exp(jnp.sum(lam4[2:3] * lam4[3:4], axis=1, keepdims=True)) + lam_init)
    o = acc_sc[...] / l_sc[...]
    o = o[:, :tile] - lam * o[:, tile:]
    ms = jnp.mean(o * o, axis=0, keepdims=True)
    y_t = o * lax.rsqrt(ms + LN_EPS) * g_ref[...] * (1.0 - lam_init)
    o_ref[...] = y_t.T


def _attention(proj, bsz, seq, layer, far_bias, bias_tiles, lam4, subln_g_col, lam_init):
    tile = min(ATT_TILE, seq)
    nq = seq // tile
    qcol = 2 * D_RNN // HEAD_V
    kcol = qcol + D_ATT // HEAD_V
    vcol = kcol + D_ATT // HEAD_V
    grid_spec = pltpu.PrefetchScalarGridSpec(
        num_scalar_prefetch=1,
        grid=(bsz, N_HEADS, nq),
        in_specs=[pl.BlockSpec((tile, HEAD_V), lambda b, h, i, far: (b * nq + i, qcol + h)),
                  pl.BlockSpec((seq, HEAD_V), lambda b, h, i, far: (b, kcol + h)),
                  pl.BlockSpec((seq, HEAD_V), lambda b, h, i, far: (b, vcol + h)),
                  pl.BlockSpec((None, 2, tile, 2 * tile), lambda b, h, i, far: (h, 0, 0, 0)),
                  pl.BlockSpec((None, 4, HEAD_QK), lambda b, h, i, far: (layer, 0, 0)),
                  pl.BlockSpec((None, HEAD_V, 1), lambda b, h, i, far: (layer, 0, 0))],
        out_specs=pl.BlockSpec((tile, HEAD_V), lambda b, h, i, far: (b * nq + i, h)),
        scratch_shapes=[pltpu.VMEM((nq, tile, HEAD_V), BF16), pltpu.VMEM((nq, HEAD_V, tile), BF16),
                        pltpu.VMEM((1, 2 * tile), F32), pltpu.VMEM((1, 2 * tile), F32),
                        pltpu.VMEM((HEAD_V, 2 * tile), F32)])
    return pl.pallas_call(
        functools.partial(_attn_kernel, tile=tile, n_kv=nq, lam_init=lam_init),
        grid_spec=grid_spec,
        out_shape=jax.ShapeDtypeStruct((bsz * seq, D_ATT), F32),
        compiler_params=_cparams(("parallel", "parallel", "arbitrary")),
        name="diff_attn",
    )(far_bias, proj, proj, proj, bias_tiles, lam4, subln_g_col)


def _t5_bucket(rel):
    half = N_BUCKETS // 2
    max_exact = half // 2
    ret = (rel > 0).astype(jnp.int32) * half
    n = jnp.abs(rel)
    large = max_exact + (jnp.log(jnp.maximum(n, 1).astype(F32) / max_exact)
                         / math.log(MAX_DIST / max_exact) * (half - max_exact)).astype(jnp.int32)
    large = jnp.minimum(large, half - 1)
    return ret + jnp.where(n < max_exact, n, large)


def _bias_tables(rel_bias, tile):
    assert tile >= MAX_DIST and tile % CHUNK == 0
    qpos = jnp.arange(tile, dtype=jnp.int32)[:, None]
    kpos = jnp.arange(tile, dtype=jnp.int32)[None, :]
    diag = rel_bias[_t5_bucket(kpos - qpos)].astype(F32)
    diag = jnp.where(((kpos // CHUNK) <= (qpos // CHUNK))[..., None], diag, NEG_INF)
    prev = rel_bias[_t5_bucket(kpos - tile - qpos)].astype(F32)
    tiles = jnp.transpose(jnp.stack([diag, prev], axis=0), (3, 0, 2, 1))
    tiles = jnp.concatenate([tiles, tiles], axis=-1)
    far = rel_bias[_t5_bucket(jnp.int32(-tile - 1))].astype(F32)
    return tiles, far


def _layer_norm(y, g, b):
    mu = jnp.mean(y, axis=-1, keepdims=True)
    d = y - mu
    var = jnp.mean(d * d, axis=-1, keepdims=True)
    return d * lax.rsqrt(var + LN_EPS) * g + b


def _out_proj_kernel(x_ref, r_ref, a_ref, w_ref, g_ref, b_ref, o_ref):
    h = jnp.dot(r_ref[...].astype(BF16), w_ref[0:D_RNN, :], preferred_element_type=F32)
    h = h + jnp.dot(a_ref[...].astype(BF16), w_ref[D_RNN:, :], preferred_element_type=F32)
    o_ref[...] = _layer_norm(ALPHA * x_ref[...] + h, g_ref[...], b_ref[...])


def _out_proj_ln(x, rnn_out, att_out, w_out_bf, ln_g, ln_b, layer, tm=256):
    t = x.shape[0]
    vec = pl.BlockSpec((None, 1, D_MODEL), lambda i: (layer, 0, 0))
    return pl.pallas_call(
        _out_proj_kernel,
        grid=(t // tm,),
        in_specs=[pl.BlockSpec((tm, D_MODEL), lambda i: (i, 0)),
                  pl.BlockSpec((tm, D_RNN), lambda i: (i, 0)),
                  pl.BlockSpec((tm, D_ATT), lambda i: (i, 0)),
                  pl.BlockSpec((None, D_MODEL, D_MODEL), lambda i: (layer, 0, 0)),
                  vec, vec],
        out_specs=pl.BlockSpec((tm, D_MODEL), lambda i: (i, 0)),
        out_shape=jax.ShapeDtypeStruct((t, D_MODEL), F32),
        compiler_params=_cparams(("parallel",)),
        name="out_proj_ln",
    )(x, rnn_out, att_out, w_out_bf, ln_g, ln_b)


def _first_index(hit, idx, size):
    return jnp.min(jnp.where(hit, idx, float(size)), axis=0, keepdims=True)


def _router_kernel(x_ref, w_ref, b_ref, e_ref, g_ref, p_ref, c_ref, cnt_sc, *, tm):
    @pl.when(pl.program_id(0) == 0)
    def _():
        cnt_sc[...] = jnp.zeros_like(cnt_sc)

    logits = lax.dot_general(w_ref[...], x_ref[...], (((1,), (1,)), ((), ())),
                             precision=lax.Precision.HIGHEST, preferred_element_type=F32)
    scores = jax.nn.sigmoid(logits)
    sel = scores + b_ref[:, 0:1]

    grp = sel.reshape(N_GROUPS, GROUP_SIZE, tm)
    gidx = lax.broadcasted_iota(jnp.int32, grp.shape, 1).astype(F32)
    m1 = jnp.max(grp, axis=1, keepdims=True)
    first = jnp.min(jnp.where(grp == m1, gidx, float(GROUP_SIZE)), axis=1, keepdims=True)
    m2 = jnp.max(jnp.where(gidx == first, -jnp.inf, grp), axis=1, keepdims=True)
    gscore = (m1 + m2).reshape(N_GROUPS, tm)

    ridx = lax.broadcasted_iota(jnp.int32, (N_GROUPS, tm), 0).astype(F32)
    gmask = jnp.zeros((N_GROUPS, tm), F32)
    for _ in range(TOPK_GROUPS):
        best = jnp.max(gscore, axis=0, keepdims=True)
        pick = ridx == _first_index(gscore == best, ridx, N_GROUPS)
        gmask = jnp.where(pick, 1.0, gmask)
        gscore = jnp.where(pick, -jnp.inf, gscore)
    emask = jnp.broadcast_to(gmask.reshape(N_GROUPS, 1, tm), (N_GROUPS, GROUP_SIZE, tm)).reshape(N_EXPERTS, tm)
    masked = jnp.where(emask > 0.0, sel, -jnp.inf)

    eidx = lax.broadcasted_iota(jnp.int32, (N_EXPERTS, tm), 0).astype(F32)
    chosen = jnp.zeros((N_EXPERTS, tm), F32)
    picks, gates = [], []
    for _ in range(TOP_K):
        best = jnp.max(masked, axis=0, keepdims=True)
        e_k = _first_index(masked == best, eidx, N_EXPERTS)
        pick = eidx == e_k
        picks.append(e_k)
        gates.append(jnp.sum(jnp.where(pick, scores, 0.0), axis=0, keepdims=True))
        chosen = jnp.where(pick, 1.0, chosen)
        masked = jnp.where(pick, -jnp.inf, masked)
    gate = jnp.concatenate(gates, axis=0)
    g_ref[...] = gate / jnp.sum(gate, axis=0, keepdims=True) * ROUTED_SCALE
    e_ref[...] = jnp.concatenate(picks, axis=0).astype(jnp.int32)

    before = (lax.broadcasted_iota(jnp.int32, (tm, tm), 0) < lax.broadcasted_iota(jnp.int32, (tm, tm), 1))
    rank = jnp.dot(chosen.astype(BF16), before.astype(BF16), preferred_element_type=F32) + cnt_sc[:, 0:1]
    p_ref[...] = jnp.concatenate(
        [jnp.sum(jnp.where(eidx == e_k, rank, 0.0), axis=0, keepdims=True) for e_k in picks],
        axis=0).astype(jnp.int32)
    cnt_sc[...] = cnt_sc[...] + jnp.sum(chosen, axis=1, keepdims=True)
    c_ref[...] = cnt_sc[...]


def _router(x, w_router_t, b_router_col, layer):
    t = x.shape[0]
    tm = min(ROUTER_TILE, t)
    return pl.pallas_call(
        functools.partial(_router_kernel, tm=tm),
        grid=(t // tm,),
        in_specs=[pl.BlockSpec((tm, D_MODEL), lambda i: (i, 0)),
                  pl.BlockSpec((None, N_EXPERTS, D_MODEL), lambda i: (layer, 0, 0)),
                  pl.BlockSpec((None, N_EXPERTS, LANES), lambda i: (layer, 0, 0))],
        out_specs=[pl.BlockSpec((TOP_K, tm), lambda i: (0, i)),
                   pl.BlockSpec((TOP_K, tm), lambda i: (0, i)),
                   pl.BlockSpec((TOP_K, tm), lambda i: (0, i)),
                   pl.BlockSpec((N_EXPERTS, LANES), lambda i: (0, 0))],
        out_shape=[jax.ShapeDtypeStruct((TOP_K, t), jnp.int32),
                   jax.ShapeDtypeStruct((TOP_K, t), F32),
                   jax.ShapeDtypeStruct((TOP_K, t), jnp.int32),
                   jax.ShapeDtypeStruct((N_EXPERTS, LANES), F32)],
        scratch_shapes=[pltpu.VMEM((N_EXPERTS, LANES), F32)],
        compiler_params=_cparams(("arbitrary",)),
        name="router",
    )(x, w_router_t, b_router_col)


def _row_gather(idx_ref, n, src_hbm, dst, sem):
    for r in range(n):
        pltpu.make_async_copy(src_hbm.at[pl.ds(idx_ref[0, 0, r], 1)], dst.at[pl.ds(r, 1)], sem).start()


def _expert_kernel(be_ref, nu_ref, tok_ref, x_hbm, wg_ref, wu_ref, wd_ref, o_ref, xbuf, sem, *, blk):
    i = pl.program_id(0)
    n_used = nu_ref[0]
    slot = 1 - i % 2

    @pl.when(i < n_used)
    def _():
        _row_gather(tok_ref, blk, x_hbm, xbuf.at[1 - slot], sem.at[1 - slot])

    @pl.when(jnp.logical_and(i >= 1, i <= n_used))
    def _():
        pltpu.make_async_copy(xbuf.at[slot], xbuf.at[slot], sem.at[slot]).wait()
        xb = xbuf[slot].astype(BF16)
        gate = jnp.dot(xb, wg_ref[...], preferred_element_type=F32)
        up = jnp.dot(xb, wu_ref[...], preferred_element_type=F32)
        hidden = (jax.nn.silu(gate) * up).astype(BF16)
        o_ref[...] = jnp.dot(hidden, wd_ref[...], preferred_element_type=F32)

    @pl.when(i > n_used)
    def _():
        o_ref[...] = jnp.zeros_like(o_ref)


def _experts(x, row_tok, block_e, n_used, wg_bf, wu_bf, wd_bf, layer, blk):
    n_blocks = row_tok.shape[0]
    last = n_blocks - 1
    prev = lambda i: jnp.maximum(i - 1, 0)
    grid_spec = pltpu.PrefetchScalarGridSpec(
        num_scalar_prefetch=2,
        grid=(n_blocks + 1,),
        in_specs=[pl.BlockSpec((1, 1, blk), lambda i, be, nu: (jnp.minimum(i, last), 0, 0),
                               memory_space=pltpu.SMEM),
                  pl.BlockSpec(memory_space=pl.ANY),
                  pl.BlockSpec((None, None, D_MODEL, D_EXPERT), lambda i, be, nu: (layer, be[prev(i)], 0, 0)),
                  pl.BlockSpec((None, None, D_MODEL, D_EXPERT), lambda i, be, nu: (layer, be[prev(i)], 0, 0)),
                  pl.BlockSpec((None, None, D_EXPERT, D_MODEL), lambda i, be, nu: (layer, be[prev(i)], 0, 0))],
        out_specs=pl.BlockSpec((blk, D_MODEL), lambda i, be, nu: (prev(i), 0)),
        scratch_shapes=[pltpu.VMEM((2, blk, D_MODEL), F32), pltpu.SemaphoreType.DMA((2,))])
    return pl.pallas_call(
        functools.partial(_expert_kernel, blk=blk),
        grid_spec=grid_spec,
        out_shape=jax.ShapeDtypeStruct((n_blocks * blk, D_MODEL), F32),
        compiler_params=_cparams(("arbitrary",)),
        name="experts",
    )(block_e, n_used, row_tok, x, wg_bf, wu_bf, wd_bf)


def _combine_kernel(dest_ref, x_ref, gate_ref, y_hbm, wg_ref, wu_ref, wd_ref, g_ref, b_ref, o_ref,
                    ybuf, sem, *, tm, n_tiles):
    i = pl.program_id(0)
    slot = 1 - i % 2

    @pl.when(i < n_tiles)
    def _():
        _row_gather(dest_ref, TOP_K * tm, y_hbm, ybuf.at[1 - slot], sem.at[1 - slot])

    @pl.when(i >= 1)
    def _():
        x = x_ref[...]
        xb = x.astype(BF16)
        hidden = (jax.nn.silu(jnp.dot(xb, wg_ref[...], preferred_element_type=F32))
                  * jnp.dot(xb, wu_ref[...], preferred_element_type=F32)).astype(BF16)
        y = ALPHA * x + jnp.dot(hidden, wd_ref[...], preferred_element_type=F32)

        pltpu.make_async_copy(ybuf.at[slot], ybuf.at[slot], sem.at[slot]).wait()
        gate = gate_ref[...]
        for k in range(TOP_K):
            y = y + gate[:, k:k + 1] * ybuf[slot, pl.ds(k * tm, tm), :]
        o_ref[...] = _layer_norm(y, g_ref[...], b_ref[...])


def _combine(x, gate_tk, dest_tiles, y_rows, wsg_bf, wsu_bf, wsd_bf, ln_g, ln_b, layer, tm):
    t = x.shape[0]
    n_tiles = t // tm
    last = n_tiles - 1
    prev = lambda i: jnp.maximum(i - 1, 0)
    vec = pl.BlockSpec((None, 1, D_MODEL), lambda i: (layer, 0, 0))
    return pl.pallas_call(
        functools.partial(_combine_kernel, tm=tm, n_tiles=n_tiles),
        grid=(n_tiles + 1,),
        in_specs=[pl.BlockSpec((1, 1, TOP_K * tm), lambda i: (jnp.minimum(i, last), 0, 0),
                               memory_space=pltpu.SMEM),
                  pl.BlockSpec((tm, D_MODEL), lambda i: (prev(i), 0)),
                  pl.BlockSpec((tm, TOP_K), lambda i: (prev(i), 0)),
                  pl.BlockSpec(memory_space=pl.ANY),
                  pl.BlockSpec((None, D_MODEL, D_EXPERT), lambda i: (layer, 0, 0)),
                  pl.BlockSpec((None, D_MODEL, D_EXPERT), lambda i: (layer, 0, 0)),
                  pl.BlockSpec((None, D_EXPERT, D_MODEL), lambda i: (layer, 0, 0)),
                  vec, vec],
        out_specs=pl.BlockSpec((tm, D_MODEL), lambda i: (prev(i), 0)),
        out_shape=jax.ShapeDtypeStruct((t, D_MODEL), F32),
        scratch_shapes=[pltpu.VMEM((2, TOP_K * tm, D_MODEL), F32), pltpu.SemaphoreType.DMA((2,))],
        compiler_params=_cparams(("arbitrary",)),
        name="combine_ln",
    )(dest_tiles, x, gate_tk, y_rows, wsg_bf, wsu_bf, wsd_bf, ln_g, ln_b)


def _dispatch_tables(top_e, rank, counts, blk, tm):
    k, t = top_e.shape
    n_blocks = (k * t) // blk + N_EXPERTS
    counts = counts[:, 0].astype(jnp.int32)
    padded = (counts + blk - 1) // blk * blk
    pend = jnp.cumsum(padded)
    pstart = pend - padded
    experts = jnp.arange(N_EXPERTS, dtype=jnp.int32)
    first_row = jnp.sum(jnp.where(top_e[None] == experts[:, None, None], pstart[:, None, None], 0), axis=0)
    dest = first_row + rank
    tok = jnp.broadcast_to(jnp.arange(t, dtype=jnp.int32)[None, :], (k, t))
    row_tok = jnp.zeros((n_blocks * blk,), jnp.int32).at[dest.reshape(-1)].set(
        tok.reshape(-1), unique_indices=True, indices_are_sorted=False)
    n_used = (pend[-1] // blk).astype(jnp.int32)
    starts = jnp.minimum(jnp.arange(n_blocks, dtype=jnp.int32), n_used - 1) * blk
    block_e = jnp.minimum(jnp.sum((pend[None, :] <= starts[:, None]).astype(jnp.int32), axis=1), N_EXPERTS - 1)
    dest_tiles = dest.reshape(k, t // tm, tm).transpose(1, 0, 2).reshape(t // tm, 1, k * tm)
    return row_tok.reshape(n_blocks, 1, blk), block_e, n_used.reshape(1), dest_tiles


def kernel(x, rel_bias, w_in, w_out, conv_w, conv_b, rg_wa, rg_ba, rg_wx, rg_bx, rg_lambda, rnn_norm_g,
           lambda_q1, lambda_k1, lambda_q2, lambda_k2, subln_g, ln1_g, ln1_b, w_router, b_router,
           w_exp_gate, w_exp_up, w_exp_down, w_sh_gate, w_sh_up, w_sh_down, ln2_g, ln2_b):
    bsz, seq, d = x.shape
    depth = w_in.shape[0]
    t = bsz * seq
    blk = min(MOE_BLOCK, t)
    ctile = min(COMBINE_TILE, t)

    w_in_bf = w_in.astype(BF16)
    w_out_bf = w_out.astype(BF16)
    wg_bf, wu_bf, wd_bf = w_exp_gate.astype(BF16), w_exp_up.astype(BF16), w_exp_down.astype(BF16)
    wsg_bf, wsu_bf, wsd_bf = w_sh_gate.astype(BF16), w_sh_up.astype(BF16), w_sh_down.astype(BF16)
    w_router_t = jnp.swapaxes(w_router, 1, 2)
    b_router_col = jnp.broadcast_to(b_router[:, :, None], (depth, N_EXPERTS, LANES))

    def pair_blocks(w):
        w = w.reshape(depth, D_RNN // LANES, 2, RNN_BLOCK, RNN_BLOCK)
        z = jnp.zeros_like(w[:, :, 0])
        top = jnp.concatenate([w[:, :, 0], z], axis=-1)
        bot = jnp.concatenate([z, w[:, :, 1]], axis=-1)
        return jnp.concatenate([top, bot], axis=-2).astype(BF16)

    wa_bd, wx_bd = pair_blocks(rg_wa), pair_blocks(rg_wx)
    row3 = lambda a: a[:, None, :]
    lam4 = jnp.stack([lambda_q1, lambda_k1, lambda_q2, lambda_k2], axis=1)
    bias_tiles, far_bias = _bias_tables(rel_bias, min(ATT_TILE, seq))

    xt = x.reshape(t, d)
    for l in range(depth):
        lam_init = 0.8 - 0.6 * math.exp(-0.3 * l)
        proj = _in_proj(xt, w_in_bf, l)
        rnn_out = _rglru(proj, bsz, seq, l, conv_w, row3(conv_b), wa_bd, row3(rg_ba), wx_bd, row3(rg_bx),
                         row3(rg_lambda), row3(rnn_norm_g))
        att_out = _attention(proj, bsz, seq, l, far_bias, bias_tiles, lam4, subln_g[:, :, None], lam_init)
        xt = _out_proj_ln(xt, rnn_out, att_out, w_out_bf, row3(ln1_g), row3(ln1_b), l)
        top_e, gate, rank, counts = _router(xt, w_router_t, b_router_col, l)
        row_tok, block_e, n_used, dest_tiles = _dispatch_tables(top_e, rank, counts, blk, ctile)
        y_rows = _experts(xt, row_tok, block_e, n_used, wg_bf, wu_bf, wd_bf, l, blk)
        xt = _combine(xt, gate.T, dest_tiles, y_rows, wsg_bf, wsu_bf, wsd_bf, row3(ln2_g), row3(ln2_b), l, ctile)
    return xt.reshape(bsz, seq, d)
```

```python
import functools
import math

import jax
import jax.numpy as jnp
from jax import lax
from jax.experimental import pallas as pl
from jax.experimental.pallas import tpu as pltpu

F32 = jnp.float32
BF16 = jnp.bfloat16

D_MODEL = 2048
MODEL_DEPTH = 4
CHUNK = 64
D_RNN = 1024
RNN_BLOCK = 64
CONV_W = 4
RG_C = 8.0
D_ATT = 1024
N_HEADS = 8
HEAD_V = 128
HEAD_QK = 64
N_BUCKETS = 32
MAX_DIST = 128
N_EXPERTS = 64
TOP_K = 8
N_GROUPS = 8
GROUP_SIZE = N_EXPERTS // N_GROUPS
TOPK_GROUPS = 4
D_EXPERT = 512
ROUTED_SCALE = 2.5
ALPHA = (2.0 * MODEL_DEPTH) ** 0.25
LN_EPS = 1e-5
NEG_INF = -1e30
LOG2_E = math.log2(math.e)
W_IN_COLS = 2 * D_RNN + 3 * D_ATT
LANES = 128
SUBLANES = 8

VMEM_LIMIT = 56 * 1024 * 1024

ATT_TILE = 512
RNN_TILE = 256
MOE_BLOCK = 256
COMBINE_TILE = 128
ROUTER_TILE = 512


def _cparams(sem):
    return pltpu.CompilerParams(dimension_semantics=sem, vmem_limit_bytes=VMEM_LIMIT)


def _matmul_kernel(x_ref, w_ref, o_ref):
    o_ref[...] = jnp.dot(x_ref[...].astype(BF16), w_ref[...], preferred_element_type=F32)


def _in_proj(x, w_in_bf, layer, tm=512, tn=1024):
    m, k = x.shape
    n = w_in_bf.shape[2]
    return pl.pallas_call(
        _matmul_kernel,
        grid=(m // tm, n // tn),
        in_specs=[pl.BlockSpec((tm, k), lambda i, j: (i, 0)),
                  pl.BlockSpec((None, k, tn), lambda i, j: (layer, 0, j))],
        out_specs=pl.BlockSpec((tm, tn), lambda i, j: (i, j)),
        out_shape=jax.ShapeDtypeStruct((m, n), F32),
        compiler_params=_cparams(("parallel", "parallel")),
        name="in_proj",
    )(x, w_in_bf)


def _rglru_kernel(xr_ref, gate_ref, cw_ref, cb_ref, wa_ref, ba_ref, wx_ref, bx_ref, lam_ref, g_ref,
                  o_ref, prev_ref, h_ref, *, ts):
    @pl.when(pl.program_id(1) == 0)
    def _():
        prev_ref[...] = jnp.zeros_like(prev_ref)
        h_ref[...] = jnp.zeros_like(h_ref)

    x = xr_ref[...]
    xe = jnp.concatenate([prev_ref[...], x], axis=0)
    xc = x * cw_ref[CONV_W - 1:CONV_W, :] + cb_ref[...]
    for k in range(1, CONV_W):
        xc = xc + pltpu.roll(xe, k, axis=0)[SUBLANES:] * cw_ref[CONV_W - 1 - k:CONV_W - k, :]
    prev_ref[...] = x[ts - SUBLANES:]

    xcb = xc.astype(BF16)
    ga, gx = [], []
    for j in range(D_RNN // LANES):
        sl = xcb[:, LANES * j:LANES * (j + 1)]
        ga.append(jnp.dot(sl, wa_ref[j], preferred_element_type=F32))
        gx.append(jnp.dot(sl, wx_ref[j], preferred_element_type=F32))
    r = jax.nn.sigmoid(jnp.concatenate(ga, axis=1) + ba_ref[...])
    gi = jax.nn.sigmoid(jnp.concatenate(gx, axis=1) + bx_ref[...])
    z = -lam_ref[...]
    softplus = jnp.maximum(z, 0.0) + jnp.log1p(jnp.exp(-jnp.abs(z)))
    log_a = (-RG_C) * r * softplus
    a = jnp.exp(log_a)
    u = jnp.sqrt(-jnp.tanh(log_a) * (a * a + 1.0)) * (gi * xc)

    row = lax.broadcasted_iota(jnp.int32, (ts, D_RNN), 0)
    k = 1
    while k < SUBLANES:
        keep = row >= k
        a_sh = jnp.where(keep, pltpu.roll(a, k, axis=0), 1.0)
        u_sh = jnp.where(keep, pltpu.roll(u, k, axis=0), 0.0)
        u = a * u_sh + u
        a = a * a_sh
        k *= 2
    while k < ts:
        u = jnp.concatenate([u[:k], a[k:] * u[:ts - k] + u[k:]], axis=0)
        a = jnp.concatenate([a[:k], a[k:] * a[:ts - k]], axis=0)
        k *= 2
    h = a * h_ref[...] + u
    h_ref[...] = h[ts - 1:ts]

    ms = jnp.mean(h * h, axis=-1, keepdims=True)
    y = h * lax.rsqrt(ms + LN_EPS) * g_ref[...]
    o_ref[...] = y * jax.nn.gelu(gate_ref[...], approximate=True)


def _rglru(proj, bsz, seq, layer, conv_w, conv_b, wa_bd, rg_ba, wx_bd, rg_bx, rg_lam, rnn_g):
    ts = min(RNN_TILE, seq)
    ns = seq // ts
    row = lambda shape: pl.BlockSpec((None,) + shape, lambda b, s: (layer,) + (0,) * len(shape))
    return pl.pallas_call(
        functools.partial(_rglru_kernel, ts=ts),
        grid=(bsz, ns),
        in_specs=[pl.BlockSpec((ts, D_RNN), lambda b, s: (b * ns + s, 0)),
                  pl.BlockSpec((ts, D_RNN), lambda b, s: (b * ns + s, 1)),
                  row((CONV_W, D_RNN)), row((1, D_RNN)),
                  row((D_RNN // LANES, LANES, LANES)), row((1, D_RNN)),
                  row((D_RNN // LANES, LANES, LANES)), row((1, D_RNN)),
                  row((1, D_RNN)), row((1, D_RNN))],
        out_specs=pl.BlockSpec((ts, D_RNN), lambda b, s: (b * ns + s, 0)),
        out_shape=jax.ShapeDtypeStruct((bsz * seq, D_RNN), F32),
        scratch_shapes=[pltpu.VMEM((SUBLANES, D_RNN), F32), pltpu.VMEM((1, D_RNN), F32)],
        compiler_params=_cparams(("parallel", "arbitrary")),
        name="rglru",
    )(proj, proj, conv_w, conv_b, wa_bd, rg_ba, wx_bd, rg_bx, rg_lam, rnn_g)


def _attn_kernel(far_ref, q_ref, k_ref, v_ref, bias_ref, lam_ref, g_ref, o_ref,
                 kbf_sc, vt_sc, m_sc, l_sc, acc_sc, *, tile, n_kv, lam_init):
    head = pl.program_id(1)
    qi = pl.program_id(2)

    @pl.when(qi == 0)
    def _():
        def prep(j, carry):
            start = pl.multiple_of(j * tile, tile)
            kbf_sc[j] = k_ref[pl.ds(start, tile), :].astype(BF16)
            vt_sc[j] = v_ref[pl.ds(start, tile), :].T.astype(BF16)
            return carry
        lax.fori_loop(0, n_kv, prep, 0)

    q_t = (q_ref[...] * (HEAD_QK ** -0.5 * LOG2_E)).T
    row = lax.broadcasted_iota(jnp.int32, (HEAD_V, tile), 0)
    q_blk = jnp.concatenate([jnp.where(row < HEAD_QK, q_t, 0.0), jnp.where(row >= HEAD_QK, q_t, 0.0)],
                            axis=1).astype(BF16)
    m_sc[...] = jnp.full_like(m_sc, NEG_INF)
    l_sc[...] = jnp.zeros_like(l_sc)
    acc_sc[...] = jnp.zeros_like(acc_sc)

    def block(j, bias, uniform_bias):
        s = jnp.dot(kbf_sc[j], q_blk, preferred_element_type=F32)
        m_old = m_sc[...]
        if uniform_bias:
            m_new = jnp.maximum(m_old, jnp.max(s, axis=0, keepdims=True) + bias)
            p = jnp.exp2(s - (m_new - bias))
        else:
            s = s + bias
            m_new = jnp.maximum(m_old, jnp.max(s, axis=0, keepdims=True))
            p = jnp.exp2(s - m_new)
        scale = jnp.exp2(m_old - m_new)
        l_sc[...] = scale * l_sc[...] + jnp.sum(p, axis=0, keepdims=True)
        acc_sc[...] = scale * acc_sc[...] + jnp.dot(vt_sc[j], p.astype(BF16), preferred_element_type=F32)
        m_sc[...] = m_new

    far_bias = far_ref[head]

    def far_body(j, carry):
        block(j, far_bias, True)
        return carry

    lax.fori_loop(0, jnp.maximum(qi - 1, 0), far_body, 0)

    @pl.when(qi >= 1)
    def _():
        block(qi - 1, bias_ref[1], False)

    block(qi, bias_ref[0], False)

    lam4 = lam_ref[...]
    lam = (jnp.exp(jnp.sum(lam4[0:1] * lam4[1:2], axis=1, keepdims=True))
           - jnp.exp(jnp.sum(lam4[2:3] * lam4[3:4], axis=1, keepdims=True)) + lam_init)
    o = acc_sc[...] / l_sc[...]
    o = o[:, :tile] - lam * o[:, tile:]
    ms = jnp.mean(o * o, axis=0, keepdims=True)
    y_t = o * lax.rsqrt(ms + LN_EPS) * g_ref[...] * (1.0 - lam_init)
    o_ref[...] = y_t.T


def _attention(proj, bsz, seq, layer, far_bias, bias_tiles, lam4, subln_g_col, lam_init):
    tile = min(ATT_TILE, seq)
    nq = seq // tile
    qcol = 2 * D_RNN // HEAD_V
    kcol = qcol + D_ATT // HEAD_V
    vcol = kcol + D_ATT // HEAD_V
    grid_spec = pltpu.PrefetchScalarGridSpec(
        num_scalar_prefetch=1,
        grid=(bsz, N_HEADS, nq),
        in_specs=[pl.BlockSpec((tile, HEAD_V), lambda b, h, i, far: (b * nq + i, qcol + h)),
                  pl.BlockSpec((seq, HEAD_V), lambda b, h, i, far: (b, kcol + h)),
                  pl.BlockSpec((seq, HEAD_V), lambda b, h, i, far: (b, vcol + h)),
                  pl.BlockSpec((None, 2, tile, 2 * tile), lambda b, h, i, far: (h, 0, 0, 0)),
                  pl.BlockSpec((None, 4, HEAD_QK), lambda b, h, i, far: (layer, 0, 0)),
                  pl.BlockSpec((None, HEAD_V, 1), lambda b, h, i, far: (layer, 0, 0))],
        out_specs=pl.BlockSpec((tile, HEAD_V), lambda b, h, i, far: (b * nq + i, h)),
        scratch_shapes=[pltpu.VMEM((nq, tile, HEAD_V), BF16), pltpu.VMEM((nq, HEAD_V, tile), BF16),
                        pltpu.VMEM((1, 2 * tile), F32), pltpu.VMEM((1, 2 * tile), F32),
                        pltpu.VMEM((HEAD_V, 2 * tile), F32)])
    return pl.pallas_call(
        functools.partial(_attn_kernel, tile=tile, n_kv=nq, lam_init=lam_init),
        grid_spec=grid_spec,
        out_shape=jax.ShapeDtypeStruct((bsz * seq, D_ATT), F32),
        compiler_params=_cparams(("parallel", "parallel", "arbitrary")),
        name="diff_attn",
    )(far_bias, proj, proj, proj, bias_tiles, lam4, subln_g_col)


def _t5_bucket(rel):
    half = N_BUCKETS // 2
    max_exact = half // 2
    ret = (rel > 0).astype(jnp.int32) * half
    n = jnp.abs(rel)
    large = max_exact + (jnp.log(jnp.maximum(n, 1).astype(F32) / max_exact)
                         / math.log(MAX_DIST / max_exact) * (half - max_exact)).astype(jnp.int32)
    large = jnp.minimum(large, half - 1)
    return ret + jnp.where(n < max_exact, n, large)


def _bias_tables(rel_bias, tile):
    assert tile >= MAX_DIST and tile % CHUNK == 0
    qpos = jnp.arange(tile, dtype=jnp.int32)[:, None]
    kpos = jnp.arange(tile, dtype=jnp.int32)[None, :]

    def lookup(bucket):
        out = jnp.zeros(bucket.shape + (N_HEADS,), F32)
        for b in range(N_BUCKETS):
            out = jnp.where((bucket == b)[..., None], rel_bias[b].astype(F32), out)
        return out

    diag = lookup(_t5_bucket(kpos - qpos))
    diag = jnp.where(((kpos // CHUNK) <= (qpos // CHUNK))[..., None], diag, NEG_INF)
    prev = lookup(_t5_bucket(kpos - tile - qpos))
    tiles = jnp.transpose(jnp.stack([diag, prev], axis=0), (3, 0, 2, 1))
    tiles = jnp.concatenate([tiles, tiles], axis=-1)
    far = rel_bias[_t5_bucket(jnp.int32(-tile - 1))].astype(F32)
    return tiles * LOG2_E, far * LOG2_E


def _layer_norm(y, g, b):
    mu = jnp.mean(y, axis=-1, keepdims=True)
    d = y - mu
    var = jnp.mean(d * d, axis=-1, keepdims=True)
    return d * lax.rsqrt(var + LN_EPS) * g + b


def _out_proj_kernel(x_ref, r_ref, a_ref, w_ref, g_ref, b_ref, o_ref):
    h = jnp.dot(r_ref[...].astype(BF16), w_ref[0:D_RNN, :], preferred_element_type=F32)
    h = h + jnp.dot(a_ref[...].astype(BF16), w_ref[D_RNN:, :], preferred_element_type=F32)
    o_ref[...] = _layer_norm(ALPHA * x_ref[...] + h, g_ref[...], b_ref[...])


def _out_proj_ln(x, rnn_out, att_out, w_out_bf, ln_g, ln_b, layer, tm=256):
    t = x.shape[0]
    vec = pl.BlockSpec((None, 1, D_MODEL), lambda i: (layer, 0, 0))
    return pl.pallas_call(
        _out_proj_kernel,
        grid=(t // tm,),
        in_specs=[pl.BlockSpec((tm, D_MODEL), lambda i: (i, 0)),
                  pl.BlockSpec((tm, D_RNN), lambda i: (i, 0)),
                  pl.BlockSpec((tm, D_ATT), lambda i: (i, 0)),
                  pl.BlockSpec((None, D_MODEL, D_MODEL), lambda i: (layer, 0, 0)),
                  vec, vec],
        out_specs=pl.BlockSpec((tm, D_MODEL), lambda i: (i, 0)),
        out_shape=jax.ShapeDtypeStruct((t, D_MODEL), F32),
        compiler_params=_cparams(("parallel",)),
        name="out_proj_ln",
    )(x, rnn_out, att_out, w_out_bf, ln_g, ln_b)


def _first_index(hit, idx, size):
    return jnp.min(jnp.where(hit, idx, float(size)), axis=0, keepdims=True)


def _router_kernel(x_ref, w_ref, b_ref, e_ref, g_ref, p_ref, c_ref, cnt_sc, *, tm):
    @pl.when(pl.program_id(0) == 0)
    def _():
        cnt_sc[...] = jnp.zeros_like(cnt_sc)

    logits = lax.dot_general(w_ref[...], x_ref[...], (((1,), (1,)), ((), ())),
                             precision=lax.Precision.HIGHEST, preferred_element_type=F32)
    scores = jax.nn.sigmoid(logits)
    sel = scores + b_ref[:, 0:1]

    grp = sel.reshape(N_GROUPS, GROUP_SIZE, tm)
    gidx = lax.broadcasted_iota(jnp.int32, grp.shape, 1).astype(F32)
    m1 = jnp.max(grp, axis=1, keepdims=True)
    first = jnp.min(jnp.where(grp == m1, gidx, float(GROUP_SIZE)), axis=1, keepdims=True)
    m2 = jnp.max(jnp.where(gidx == first, -jnp.inf, grp), axis=1, keepdims=True)
    gscore = (m1 + m2).reshape(N_GROUPS, tm)

    ridx = lax.broadcasted_iota(jnp.int32, (N_GROUPS, tm), 0).astype(F32)
    gmask = jnp.zeros((N_GROUPS, tm), F32)
    for _ in range(TOPK_GROUPS):
        best = jnp.max(gscore, axis=0, keepdims=True)
        pick = ridx == _first_index(gscore == best, ridx, N_GROUPS)
        gmask = jnp.where(pick, 1.0, gmask)
        gscore = jnp.where(pick, -jnp.inf, gscore)
    emask = jnp.broadcast_to(gmask.reshape(N_GROUPS, 1, tm), (N_GROUPS, GROUP_SIZE, tm)).reshape(N_EXPERTS, tm)
    masked = jnp.where(emask > 0.0, sel, -jnp.inf)

    eidx = lax.broadcasted_iota(jnp.int32, (N_EXPERTS, tm), 0).astype(F32)
    chosen = jnp.zeros((N_EXPERTS, tm), F32)
    picks, gates = [], []
    for _ in range(TOP_K):
        best = jnp.max(masked, axis=0, keepdims=True)
        e_k = _first_index(masked == best, eidx, N_EXPERTS)
        pick = eidx == e_k
        picks.append(e_k)
        gates.append(jnp.sum(jnp.where(pick, scores, 0.0), axis=0, keepdims=True))
        chosen = jnp.where(pick, 1.0, chosen)
        masked = jnp.where(pick, -jnp.inf, masked)
    gate = jnp.concatenate(gates, axis=0)
    g_ref[...] = gate / jnp.sum(gate, axis=0, keepdims=True) * ROUTED_SCALE
    e_ref[...] = jnp.concatenate(picks, axis=0).astype(jnp.int32)

    before = (lax.broadcasted_iota(jnp.int32, (tm, tm), 0) < lax.broadcasted_iota(jnp.int32, (tm, tm), 1))
    rank = jnp.dot(chosen.astype(BF16), before.astype(BF16), preferred_element_type=F32) + cnt_sc[:, 0:1]
    p_ref[...] = jnp.concatenate(
        [jnp.sum(jnp.where(eidx == e_k, rank, 0.0), axis=0, keepdims=True) for e_k in picks],
        axis=0).astype(jnp.int32)
    cnt_sc[...] = cnt_sc[...] + jnp.sum(chosen, axis=1, keepdims=True)
    c_ref[...] = cnt_sc[...]


def _router(x, w_router_t, b_router_col, layer):
    t = x.shape[0]
    tm = min(ROUTER_TILE, t)
    return pl.pallas_call(
        functools.partial(_router_kernel, tm=tm),
        grid=(t // tm,),
        in_specs=[pl.BlockSpec((tm, D_MODEL), lambda i: (i, 0)),
                  pl.BlockSpec((None, N_EXPERTS, D_MODEL), lambda i: (layer, 0, 0)),
                  pl.BlockSpec((None, N_EXPERTS, LANES), lambda i: (layer, 0, 0))],
        out_specs=[pl.BlockSpec((TOP_K, tm), lambda i: (0, i)),
                   pl.BlockSpec((TOP_K, tm), lambda i: (0, i)),
                   pl.BlockSpec((TOP_K, tm), lambda i: (0, i)),
                   pl.BlockSpec((N_EXPERTS, LANES), lambda i: (0, 0))],
        out_shape=[jax.ShapeDtypeStruct((TOP_K, t), jnp.int32),
                   jax.ShapeDtypeStruct((TOP_K, t), F32),
                   jax.ShapeDtypeStruct((TOP_K, t), jnp.int32),
                   jax.ShapeDtypeStruct((N_EXPERTS, LANES), F32)],
        scratch_shapes=[pltpu.VMEM((N_EXPERTS, LANES), F32)],
        compiler_params=_cparams(("arbitrary",)),
        name="router",
    )(x, w_router_t, b_router_col)


SLAB_ROWS = D_MODEL // LANES
SLAB_PITCH = SLAB_ROWS + 1


def _slab_gather(idx_ref, n, src_hbm, dst, sem):
    for r in range(n):
        pltpu.make_async_copy(src_hbm.at[idx_ref[0, 0, r]], dst.at[pl.ds(r * SLAB_PITCH, SLAB_ROWS)], sem).start()


def _slab_wait(buf, n, sem):
    view = buf.at[pl.ds(0, n * SLAB_ROWS)]
    pltpu.make_async_copy(view, view, sem).wait()


def _slab_chunk(buf, first, n, c):
    return buf[pl.ds(first * SLAB_PITCH + c, n, stride=SLAB_PITCH), :]


def _expert_kernel(be_ref, nu_ref, tok_ref, x_hbm, wg_ref, wu_ref, wd_ref, o_ref, xbuf, sem, *, blk):
    i = pl.program_id(0)
    n_used = nu_ref[0]
    slot = 1 - i % 2

    @pl.when(i < n_used)
    def _():
        _slab_gather(tok_ref, blk, x_hbm, xbuf.at[1 - slot], sem.at[1 - slot])

    @pl.when(jnp.logical_and(i >= 1, i <= n_used))
    def _():
        _slab_wait(xbuf.at[slot], blk, sem.at[slot])
        xb = jnp.concatenate([_slab_chunk(xbuf.at[slot], 0, blk, c) for c in range(SLAB_ROWS)],
                             axis=1).astype(BF16)
        gate = jnp.dot(xb, wg_ref[...], preferred_element_type=F32)
        up = jnp.dot(xb, wu_ref[...], preferred_element_type=F32)
        hidden = (jax.nn.silu(gate) * up).astype(BF16)
        y = jnp.dot(hidden, wd_ref[...], preferred_element_type=F32)
        for c in range(SLAB_ROWS):
            o_ref[pl.ds(c, blk, stride=SLAB_ROWS), :] = y[:, c * LANES:(c + 1) * LANES]

    @pl.when(i > n_used)
    def _():
        o_ref[...] = jnp.zeros_like(o_ref)


def _experts(x_slabs, row_tok, block_e, n_used, wg_bf, wu_bf, wd_bf, layer, blk):
    n_blocks = row_tok.shape[0]
    last = n_blocks - 1
    prev = lambda i: jnp.maximum(i - 1, 0)
    grid_spec = pltpu.PrefetchScalarGridSpec(
        num_scalar_prefetch=2,
        grid=(n_blocks + 1,),
        in_specs=[pl.BlockSpec((1, 1, blk), lambda i, be, nu: (jnp.minimum(i, last), 0, 0),
                               memory_space=pltpu.SMEM),
                  pl.BlockSpec(memory_space=pl.ANY),
                  pl.BlockSpec((None, None, D_MODEL, D_EXPERT), lambda i, be, nu: (layer, be[prev(i)], 0, 0)),
                  pl.BlockSpec((None, None, D_MODEL, D_EXPERT), lambda i, be, nu: (layer, be[prev(i)], 0, 0)),
                  pl.BlockSpec((None, None, D_EXPERT, D_MODEL), lambda i, be, nu: (layer, be[prev(i)], 0, 0))],
        out_specs=pl.BlockSpec((blk * SLAB_ROWS, LANES), lambda i, be, nu: (prev(i), 0)),
        scratch_shapes=[pltpu.VMEM((2, blk * SLAB_PITCH, LANES), F32), pltpu.SemaphoreType.DMA((2,))])
    y = pl.pallas_call(
        functools.partial(_expert_kernel, blk=blk),
        grid_spec=grid_spec,
        out_shape=jax.ShapeDtypeStruct((n_blocks * blk * SLAB_ROWS, LANES), F32),
        compiler_params=_cparams(("arbitrary",)),
        name="experts",
    )(block_e, n_used, row_tok, x_slabs, wg_bf, wu_bf, wd_bf)
    return y.reshape(n_blocks * blk, SLAB_ROWS, LANES)


def _combine_kernel(dest_ref, x_ref, gate_ref, y_hbm, wg_ref, wu_ref, wd_ref, g_ref, b_ref, o_ref,
                    ybuf, sem, *, tm, n_tiles):
    i = pl.program_id(0)
    slot = 1 - i % 2

    @pl.when(i < n_tiles)
    def _():
        _slab_gather(dest_ref, TOP_K * tm, y_hbm, ybuf.at[1 - slot], sem.at[1 - slot])

    @pl.when(i >= 1)
    def _():
        x = x_ref[...]
        xb = x.astype(BF16)
        hidden = (jax.nn.silu(jnp.dot(xb, wg_ref[...], preferred_element_type=F32))
                  * jnp.dot(xb, wu_ref[...], preferred_element_type=F32)).astype(BF16)
        y = ALPHA * x + jnp.dot(hidden, wd_ref[...], preferred_element_type=F32)

        _slab_wait(ybuf.at[slot], TOP_K * tm, sem.at[slot])
        gate = gate_ref[...]
        gates = [gate[:, k:k + 1] for k in range(TOP_K)]
        routed = []
        for c in range(SLAB_ROWS):
            acc = gates[0] * _slab_chunk(ybuf.at[slot], 0, tm, c)
            for k in range(1, TOP_K):
                acc = acc + gates[k] * _slab_chunk(ybuf.at[slot], k * tm, tm, c)
            routed.append(acc)
        y = y + jnp.concatenate(routed, axis=1)
        o_ref[...] = _layer_norm(y, g_ref[...], b_ref[...])


def _combine(x, gate_tk, dest_tiles, y_rows, wsg_bf, wsu_bf, wsd_bf, ln_g, ln_b, layer, tm):
    t = x.shape[0]
    n_tiles = t // tm
    last = n_tiles - 1
    prev = lambda i: jnp.maximum(i - 1, 0)
    vec = pl.BlockSpec((None, 1, D_MODEL), lambda i: (layer, 0, 0))
    return pl.pallas_call(
        functools.partial(_combine_kernel, tm=tm, n_tiles=n_tiles),
        grid=(n_tiles + 1,),
        in_specs=[pl.BlockSpec((1, 1, TOP_K * tm), lambda i: (jnp.minimum(i, last), 0, 0),
                               memory_space=pltpu.SMEM),
                  pl.BlockSpec((tm, D_MODEL), lambda i: (prev(i), 0)),
                  pl.BlockSpec((tm, TOP_K), lambda i: (prev(i), 0)),
                  pl.BlockSpec(memory_space=pl.ANY),
                  pl.BlockSpec((None, D_MODEL, D_EXPERT), lambda i: (layer, 0, 0)),
                  pl.BlockSpec((None, D_MODEL, D_EXPERT), lambda i: (layer, 0, 0)),
                  pl.BlockSpec((None, D_EXPERT, D_MODEL), lambda i: (layer, 0, 0)),
                  vec, vec],
        out_specs=pl.BlockSpec((tm, D_MODEL), lambda i: (prev(i), 0)),
        out_shape=jax.ShapeDtypeStruct((t, D_MODEL), F32),
        scratch_shapes=[pltpu.VMEM((2, TOP_K * tm * SLAB_PITCH, LANES), F32), pltpu.SemaphoreType.DMA((2,))],
        compiler_params=_cparams(("arbitrary",)),
        name="combine_ln",
    )(dest_tiles, x, gate_tk, y_rows, wsg_bf, wsu_bf, wsd_bf, ln_g, ln_b)


def _dispatch_tables(top_e, rank, counts, blk, tm):
    k, t = top_e.shape
    n_blocks = (k * t) // blk + N_EXPERTS
    counts = counts[:, 0].astype(jnp.int32)
    padded = (counts + blk - 1) // blk * blk
    pend = jnp.cumsum(padded)
    pstart = pend - padded
    experts = jnp.arange(N_EXPERTS, dtype=jnp.int32)
    first_row = jnp.sum(jnp.where(top_e[None] == experts[:, None, None], pstart[:, None, None], 0), axis=0)
    dest = first_row + rank
    tok = jnp.broadcast_to(jnp.arange(t, dtype=jnp.int32)[None, :], (k, t))
    row_tok = jnp.zeros((n_blocks * blk,), jnp.int32).at[dest.reshape(-1)].set(
        tok.reshape(-1), unique_indices=True, indices_are_sorted=False)
    n_used = (pend[-1] // blk).astype(jnp.int32)
    starts = jnp.minimum(jnp.arange(n_blocks, dtype=jnp.int32), n_used - 1) * blk
    block_e = jnp.minimum(jnp.sum((pend[None, :] <= starts[:, None]).astype(jnp.int32), axis=1), N_EXPERTS - 1)
    dest_tiles = dest.reshape(k, t // tm, tm).transpose(1, 0, 2).reshape(t // tm, 1, k * tm)
    return row_tok.reshape(n_blocks, 1, blk), block_e, n_used.reshape(1), dest_tiles


def kernel(x, rel_bias, w_in, w_out, conv_w, conv_b, rg_wa, rg_ba, rg_wx, rg_bx, rg_lambda, rnn_norm_g,
           lambda_q1, lambda_k1, lambda_q2, lambda_k2, subln_g, ln1_g, ln1_b, w_router, b_router,
           w_exp_gate, w_exp_up, w_exp_down, w_sh_gate, w_sh_up, w_sh_down, ln2_g, ln2_b):
    bsz, seq, d = x.shape
    depth = w_in.shape[0]
    t = bsz * seq
    blk = min(MOE_BLOCK, t)
    ctile = min(COMBINE_TILE, t)

    w_in_bf = w_in.astype(BF16)
    w_out_bf = w_out.astype(BF16)
    wg_bf, wu_bf, wd_bf = w_exp_gate.astype(BF16), w_exp_up.astype(BF16), w_exp_down.astype(BF16)
    wsg_bf, wsu_bf, wsd_bf = w_sh_gate.astype(BF16), w_sh_up.astype(BF16), w_sh_down.astype(BF16)
    w_router_t = jnp.swapaxes(w_router, 1, 2)
    b_router_col = jnp.broadcast_to(b_router[:, :, None], (depth, N_EXPERTS, LANES))

    def pair_blocks(w):
        w = w.reshape(depth, D_RNN // LANES, 2, RNN_BLOCK, RNN_BLOCK)
        z = jnp.zeros_like(w[:, :, 0])
        top = jnp.concatenate([w[:, :, 0], z], axis=-1)
        bot = jnp.concatenate([z, w[:, :, 1]], axis=-1)
        return jnp.concatenate([top, bot], axis=-2).astype(BF16)

    wa_bd, wx_bd = pair_blocks(rg_wa), pair_blocks(rg_wx)
    row3 = lambda a: a[:, None, :]
    lam4 = jnp.stack([lambda_q1, lambda_k1, lambda_q2, lambda_k2], axis=1)
    bias_tiles, far_bias = _bias_tables(rel_bias, min(ATT_TILE, seq))

    xt = x.reshape(t, d)
    for l in range(depth):
        lam_init = 0.8 - 0.6 * math.exp(-0.3 * l)
        proj = _in_proj(xt, w_in_bf, l)
        rnn_out = _rglru(proj, bsz, seq, l, conv_w, row3(conv_b), wa_bd, row3(rg_ba), wx_bd, row3(rg_bx),
                         row3(rg_lambda), row3(rnn_norm_g))
        att_out = _attention(proj, bsz, seq, l, far_bias, bias_tiles, lam4, subln_g[:, :, None], lam_init)
        xt = _out_proj_ln(xt, rnn_out, att_out, w_out_bf, row3(ln1_g), row3(ln1_b), l)
        top_e, gate, rank, counts = _router(xt, w_router_t, b_router_col, l)
        row_tok, block_e, n_used, dest_tiles = _dispatch_tables(top_e, rank, counts, blk, ctile)
        y_rows = _experts(xt.reshape(t, SLAB_ROWS, LANES), row_tok, block_e, n_used, wg_bf, wu_bf, wd_bf, l, blk)
        xt = _combine(xt, gate.T, dest_tiles, y_rows, wsg_bf, wsu_bf, wsd_bf, row3(ln2_g), row3(ln2_b), l, ctile)
    return xt.reshape(bsz, seq, d)
```

```python
import functools
import math

import jax
import jax.numpy as jnp
from jax import lax
from jax.experimental import pallas as pl
from jax.experimental.pallas import tpu as pltpu

F32 = jnp.float32
BF16 = jnp.bfloat16

D_MODEL = 2048
MODEL_DEPTH = 4
CHUNK = 64
D_RNN = 1024
RNN_BLOCK = 64
CONV_W = 4
RG_C = 8.0
D_ATT = 1024
N_HEADS = 8
HEAD_V = 128
HEAD_QK = 64
N_BUCKETS = 32
MAX_DIST = 128
N_EXPERTS = 64
TOP_K = 8
N_GROUPS = 8
GROUP_SIZE = N_EXPERTS // N_GROUPS
TOPK_GROUPS = 4
D_EXPERT = 512
ROUTED_SCALE = 2.5
ALPHA = (2.0 * MODEL_DEPTH) ** 0.25
LN_EPS = 1e-5
NEG_INF = -1e30
LOG2_E = math.log2(math.e)
W_IN_COLS = 2 * D_RNN + 3 * D_ATT
LANES = 128
SUBLANES = 8

VMEM_LIMIT = 56 * 1024 * 1024

ATT_TILE = 512
ATT_HEADS = 2
RNN_TILE = 256
MOE_BLOCK = 256
COMBINE_TILE = 128
ROUTER_TILE = 512


def _cparams(sem):
    return pltpu.CompilerParams(dimension_semantics=sem, vmem_limit_bytes=VMEM_LIMIT)


def _matmul_kernel(x_ref, w_ref, o_ref):
    o_ref[...] = jnp.dot(x_ref[...], w_ref[...], preferred_element_type=F32).astype(o_ref.dtype)


def _in_proj(x_bf, w_in_bf, layer, first_col, n, out_dtype, tn=1024):
    m, k = x_bf.shape
    tm = min(1024, m)
    j0 = first_col // tn
    return pl.pallas_call(
        _matmul_kernel,
        grid=(n // tn, m // tm),
        in_specs=[pl.BlockSpec((tm, k), lambda j, i: (i, 0)),
                  pl.BlockSpec((None, k, tn), lambda j, i: (layer, 0, j0 + j))],
        out_specs=pl.BlockSpec((tm, tn), lambda j, i: (i, j)),
        out_shape=jax.ShapeDtypeStruct((m, n), out_dtype),
        compiler_params=_cparams(("parallel", "parallel")),
        name="in_proj",
    )(x_bf, w_in_bf)


def _rglru_kernel(xr_ref, gate_ref, cw_ref, cb_ref, wa_ref, ba_ref, wx_ref, bx_ref, lam_ref, g_ref,
                  o_ref, prev_ref, h_ref, *, ts):
    @pl.when(pl.program_id(1) == 0)
    def _():
        prev_ref[...] = jnp.zeros_like(prev_ref)
        h_ref[...] = jnp.zeros_like(h_ref)

    x = xr_ref[...]
    xe = jnp.concatenate([prev_ref[...], x], axis=0)
    xc = x * cw_ref[CONV_W - 1:CONV_W, :] + cb_ref[...]
    for k in range(1, CONV_W):
        xc = xc + pltpu.roll(xe, k, axis=0)[SUBLANES:] * cw_ref[CONV_W - 1 - k:CONV_W - k, :]
    prev_ref[...] = x[ts - SUBLANES:]

    xcb = xc.astype(BF16)
    ga, gx = [], []
    for j in range(D_RNN // LANES):
        sl = xcb[:, LANES * j:LANES * (j + 1)]
        ga.append(jnp.dot(sl, wa_ref[j], preferred_element_type=F32))
        gx.append(jnp.dot(sl, wx_ref[j], preferred_element_type=F32))
    r = jax.nn.sigmoid(jnp.concatenate(ga, axis=1) + ba_ref[...])
    gi = jax.nn.sigmoid(jnp.concatenate(gx, axis=1) + bx_ref[...])
    z = -lam_ref[...]
    softplus = jnp.maximum(z, 0.0) + jnp.log1p(jnp.exp(-jnp.abs(z)))
    log_a = (-RG_C) * r * softplus
    a = jnp.exp(log_a)
    u = jnp.sqrt(-jnp.tanh(log_a) * (a * a + 1.0)) * (gi * xc)

    row = lax.broadcasted_iota(jnp.int32, (ts, D_RNN), 0)
    k = 1
    while k < SUBLANES:
        keep = row >= k
        a_sh = jnp.where(keep, pltpu.roll(a, k, axis=0), 1.0)
        u_sh = jnp.where(keep, pltpu.roll(u, k, axis=0), 0.0)
        u = a * u_sh + u
        a = a * a_sh
        k *= 2
    while k < ts:
        u = jnp.concatenate([u[:k], a[k:] * u[:ts - k] + u[k:]], axis=0)
        a = jnp.concatenate([a[:k], a[k:] * a[:ts - k]], axis=0)
        k *= 2
    h = a * h_ref[...] + u
    h_ref[...] = h[ts - 1:ts]

    ms = jnp.mean(h * h, axis=-1, keepdims=True)
    y = h * lax.rsqrt(ms + LN_EPS) * g_ref[...]
    o_ref[...] = y * jax.nn.gelu(gate_ref[...], approximate=True)


def _rglru(proj, bsz, seq, layer, conv_w, conv_b, wa_bd, rg_ba, wx_bd, rg_bx, rg_lam, rnn_g):
    ts = min(RNN_TILE, seq)
    ns = seq // ts
    row = lambda shape: pl.BlockSpec((None,) + shape, lambda b, s: (layer,) + (0,) * len(shape))
    return pl.pallas_call(
        functools.partial(_rglru_kernel, ts=ts),
        grid=(bsz, ns),
        in_specs=[pl.BlockSpec((ts, D_RNN), lambda b, s: (b * ns + s, 0)),
                  pl.BlockSpec((ts, D_RNN), lambda b, s: (b * ns + s, 1)),
                  row((CONV_W, D_RNN)), row((1, D_RNN)),
                  row((D_RNN // LANES, LANES, LANES)), row((1, D_RNN)),
                  row((D_RNN // LANES, LANES, LANES)), row((1, D_RNN)),
                  row((1, D_RNN)), row((1, D_RNN))],
        out_specs=pl.BlockSpec((ts, D_RNN), lambda b, s: (b * ns + s, 0)),
        out_shape=jax.ShapeDtypeStruct((bsz * seq, D_RNN), F32),
        scratch_shapes=[pltpu.VMEM((SUBLANES, D_RNN), F32), pltpu.VMEM((1, D_RNN), F32)],
        compiler_params=_cparams(("parallel", "arbitrary")),
        name="rglru",
    )(proj, proj, conv_w, conv_b, wa_bd, rg_ba, wx_bd, rg_bx, rg_lam, rnn_g)


def _attn_kernel(far_ref, q_ref, k_ref, v_ref, bias_ref, lam_ref, g_ref, o_ref,
                 vt_sc, m_sc, l_sc, acc_sc, *, tile, n_kv, lam_init):
    group = pl.program_id(1)
    qi = pl.program_id(2)
    heads = range(ATT_HEADS)
    cols = lambda g: slice(g * HEAD_V, (g + 1) * HEAD_V)

    @pl.when(qi == 0)
    def _():
        def prep(j, carry):
            start = pl.multiple_of(j * tile, tile)
            v = v_ref[pl.ds(start, tile), :].astype(F32)
            for g in heads:
                vt_sc[g, j] = v[:, cols(g)].T.astype(BF16)
            return carry
        lax.fori_loop(0, n_kv, prep, 0)

    row = lax.broadcasted_iota(jnp.int32, (HEAD_V, tile), 0)
    q_blk = []
    for g in heads:
        q_t = (q_ref[:, cols(g)] * (HEAD_QK ** -0.5 * LOG2_E)).T
        q_blk.append(jnp.concatenate([jnp.where(row < HEAD_QK, q_t, 0.0), jnp.where(row >= HEAD_QK, q_t, 0.0)],
                                     axis=1).astype(BF16))
    m_sc[...] = jnp.full_like(m_sc, NEG_INF)
    l_sc[...] = jnp.zeros_like(l_sc)
    acc_sc[...] = jnp.zeros_like(acc_sc)

    def block(j, biases, uniform_bias):
        start = pl.multiple_of(j * tile, tile)
        kb = k_ref[pl.ds(start, tile), :]
        for g in heads:
            bias = biases[g]
            s = jnp.dot(kb[:, cols(g)], q_blk[g], preferred_element_type=F32)
            m_old = m_sc[g]
            if uniform_bias:
                m_new = jnp.maximum(m_old, jnp.max(s, axis=0, keepdims=True) + bias)
                p = jnp.exp2(s - (m_new - bias))
            else:
                s = s + bias
                m_new = jnp.maximum(m_old, jnp.max(s, axis=0, keepdims=True))
                p = jnp.exp2(s - m_new)
            scale = jnp.exp2(m_old - m_new)
            l_sc[g] = scale * l_sc[g] + jnp.sum(p, axis=0, keepdims=True)
            acc_sc[g] = scale * acc_sc[g] + jnp.dot(vt_sc[g, j], p.astype(BF16), preferred_element_type=F32)
            m_sc[g] = m_new

    far_bias = [far_ref[group * ATT_HEADS + g] for g in heads]

    def far_body(j, carry):
        block(j, far_bias, True)
        return carry

    lax.fori_loop(0, jnp.maximum(qi - 1, 0), far_body, 0)

    @pl.when(qi >= 1)
    def _():
        block(qi - 1, [bias_ref[g, 1] for g in heads], False)

    block(qi, [bias_ref[g, 0] for g in heads], False)

    lam4 = lam_ref[...]
    lam = (jnp.exp(jnp.sum(lam4[0:1] * lam4[1:2], axis=1, keepdims=True))
           - jnp.exp(jnp.sum(lam4[2:3] * lam4[3:4], axis=1, keepdims=True)) + lam_init)
    for g in heads:
        o = acc_sc[g] / l_sc[g]
        o = o[:, :tile] - lam * o[:, tile:]
        ms = jnp.mean(o * o, axis=0, keepdims=True)
        y_t = o * lax.rsqrt(ms + LN_EPS) * g_ref[...] * (1.0 - lam_init)
        o_ref[:, cols(g)] = y_t.T


def _attention(proj, kv, bsz, seq, layer, far_bias, bias_tiles, lam4, subln_g_col, lam_init):
    tile = min(ATT_TILE, seq)
    nq = seq // tile
    width = ATT_HEADS * HEAD_V
    qcol = 2 * D_RNN // width
    vcol = D_ATT // width
    grid_spec = pltpu.PrefetchScalarGridSpec(
        num_scalar_prefetch=1,
        grid=(bsz, N_HEADS // ATT_HEADS, nq),
        in_specs=[pl.BlockSpec((tile, width), lambda b, h, i, far: (b * nq + i, qcol + h)),
                  pl.BlockSpec((seq, width), lambda b, h, i, far: (b, h)),
                  pl.BlockSpec((seq, width), lambda b, h, i, far: (b, vcol + h)),
                  pl.BlockSpec((ATT_HEADS, 2, tile, 2 * tile), lambda b, h, i, far: (h, 0, 0, 0)),
                  pl.BlockSpec((None, 4, HEAD_QK), lambda b, h, i, far: (layer, 0, 0)),
                  pl.BlockSpec((None, HEAD_V, 1), lambda b, h, i, far: (layer, 0, 0))],
        out_specs=pl.BlockSpec((tile, width), lambda b, h, i, far: (b * nq + i, h)),
        scratch_shapes=[pltpu.VMEM((ATT_HEADS, nq, HEAD_V, tile), BF16),
                        pltpu.VMEM((ATT_HEADS, 1, 2 * tile), F32), pltpu.VMEM((ATT_HEADS, 1, 2 * tile), F32),
                        pltpu.VMEM((ATT_HEADS, HEAD_V, 2 * tile), F32)])
    return pl.pallas_call(
        functools.partial(_attn_kernel, tile=tile, n_kv=nq, lam_init=lam_init),
        grid_spec=grid_spec,
        out_shape=jax.ShapeDtypeStruct((bsz * seq, D_ATT), F32),
        compiler_params=_cparams(("parallel", "parallel", "arbitrary")),
        name="diff_attn",
    )(far_bias, proj, kv, kv, bias_tiles, lam4, subln_g_col)


def _t5_bucket(rel):
    half = N_BUCKETS // 2
    max_exact = half // 2
    ret = (rel > 0).astype(jnp.int32) * half
    n = jnp.abs(rel)
    large = max_exact + (jnp.log(jnp.maximum(n, 1).astype(F32) / max_exact)
                         / math.log(MAX_DIST / max_exact) * (half - max_exact)).astype(jnp.int32)
    large = jnp.minimum(large, half - 1)
    return ret + jnp.where(n < max_exact, n, large)


def _bias_tables(rel_bias, tile):
    assert tile >= MAX_DIST and tile % CHUNK == 0
    qpos = jnp.arange(tile, dtype=jnp.int32)[:, None]
    kpos = jnp.arange(tile, dtype=jnp.int32)[None, :]

    def lookup(bucket):
        out = jnp.zeros(bucket.shape + (N_HEADS,), F32)
        for b in range(N_BUCKETS):
            out = jnp.where((bucket == b)[..., None], rel_bias[b].astype(F32), out)
        return out

    diag = lookup(_t5_bucket(kpos - qpos))
    diag = jnp.where(((kpos // CHUNK) <= (qpos // CHUNK))[..., None], diag, NEG_INF)
    prev = lookup(_t5_bucket(kpos - tile - qpos))
    tiles = jnp.transpose(jnp.stack([diag, prev], axis=0), (3, 0, 2, 1))
    tiles = jnp.concatenate([tiles, tiles], axis=-1)
    far = rel_bias[_t5_bucket(jnp.int32(-tile - 1))].astype(F32)
    return tiles * LOG2_E, far * LOG2_E


def _layer_norm(y, g, b):
    mu = jnp.mean(y, axis=-1, keepdims=True)
    d = y - mu
    var = jnp.mean(d * d, axis=-1, keepdims=True)
    return d * lax.rsqrt(var + LN_EPS) * g + b


def _out_proj_kernel(x_ref, r_ref, a_ref, w_ref, g_ref, b_ref, o_ref):
    h = jnp.dot(r_ref[...].astype(BF16), w_ref[0:D_RNN, :], preferred_element_type=F32)
    h = h + jnp.dot(a_ref[...].astype(BF16), w_ref[D_RNN:, :], preferred_element_type=F32)
    o_ref[...] = _layer_norm(ALPHA * x_ref[...] + h, g_ref[...], b_ref[...])


def _out_proj_ln(x, rnn_out, att_out, w_out_bf, ln_g, ln_b, layer, tm=256):
    t = x.shape[0]
    vec = pl.BlockSpec((None, 1, D_MODEL), lambda i: (layer, 0, 0))
    return pl.pallas_call(
        _out_proj_kernel,
        grid=(t // tm,),
        in_specs=[pl.BlockSpec((tm, D_MODEL), lambda i: (i, 0)),
                  pl.BlockSpec((tm, D_RNN), lambda i: (i, 0)),
                  pl.BlockSpec((tm, D_ATT), lambda i: (i, 0)),
                  pl.BlockSpec((None, D_MODEL, D_MODEL), lambda i: (layer, 0, 0)),
                  vec, vec],
        out_specs=pl.BlockSpec((tm, D_MODEL), lambda i: (i, 0)),
        out_shape=jax.ShapeDtypeStruct((t, D_MODEL), F32),
        compiler_params=_cparams(("parallel",)),
        name="out_proj_ln",
    )(x, rnn_out, att_out, w_out_bf, ln_g, ln_b)


def _first_index(hit, idx, size):
    return jnp.min(jnp.where(hit, idx, float(size)), axis=0, keepdims=True)


def _router_kernel(x_ref, w_ref, b_ref, e_ref, g_ref, p_ref, c_ref, cnt_sc, *, tm):
    @pl.when(pl.program_id(0) == 0)
    def _():
        cnt_sc[...] = jnp.zeros_like(cnt_sc)

    logits = lax.dot_general(w_ref[...], x_ref[...], (((1,), (1,)), ((), ())),
                             precision=lax.Precision.HIGHEST, preferred_element_type=F32)
    scores = jax.nn.sigmoid(logits)
    sel = scores + b_ref[:, 0:1]

    grp = sel.reshape(N_GROUPS, GROUP_SIZE, tm)
    gidx = lax.broadcasted_iota(jnp.int32, grp.shape, 1).astype(F32)
    m1 = jnp.max(grp, axis=1, keepdims=True)
    first = jnp.min(jnp.where(grp == m1, gidx, float(GROUP_SIZE)), axis=1, keepdims=True)
    m2 = jnp.max(jnp.where(gidx == first, -jnp.inf, grp), axis=1, keepdims=True)
    gscore = (m1 + m2).reshape(N_GROUPS, tm)

    ridx = lax.broadcasted_iota(jnp.int32, (N_GROUPS, tm), 0).astype(F32)
    gmask = jnp.zeros((N_GROUPS, tm), F32)
    for _ in range(TOPK_GROUPS):
        best = jnp.max(gscore, axis=0, keepdims=True)
        pick = ridx == _first_index(gscore == best, ridx, N_GROUPS)
        gmask = jnp.where(pick, 1.0, gmask)
        gscore = jnp.where(pick, -jnp.inf, gscore)
    emask = jnp.broadcast_to(gmask.reshape(N_GROUPS, 1, tm), (N_GROUPS, GROUP_SIZE, tm)).reshape(N_EXPERTS, tm)
    masked = jnp.where(emask > 0.0, sel, -jnp.inf)

    eidx = lax.broadcasted_iota(jnp.int32, (N_EXPERTS, tm), 0).astype(F32)
    chosen = jnp.zeros((N_EXPERTS, tm), F32)
    picks, gates = [], []
    for _ in range(TOP_K):
        best = jnp.max(masked, axis=0, keepdims=True)
        e_k = _first_index(masked == best, eidx, N_EXPERTS)
        pick = eidx == e_k
        picks.append(e_k)
        gates.append(jnp.sum(jnp.where(pick, scores, 0.0), axis=0, keepdims=True))
        chosen = jnp.where(pick, 1.0, chosen)
        masked = jnp.where(pick, -jnp.inf, masked)
    gate = jnp.concatenate(gates, axis=0)
    g_ref[...] = gate / jnp.sum(gate, axis=0, keepdims=True) * ROUTED_SCALE
    e_ref[...] = jnp.concatenate(picks, axis=0).astype(jnp.int32)

    before = (lax.broadcasted_iota(jnp.int32, (tm, tm), 0) < lax.broadcasted_iota(jnp.int32, (tm, tm), 1))
    rank = jnp.dot(chosen.astype(BF16), before.astype(BF16), preferred_element_type=F32) + cnt_sc[:, 0:1]
    p_ref[...] = jnp.concatenate(
        [jnp.sum(jnp.where(eidx == e_k, rank, 0.0), axis=0, keepdims=True) for e_k in picks],
        axis=0).astype(jnp.int32)
    cnt_sc[...] = cnt_sc[...] + jnp.sum(chosen, axis=1, keepdims=True)
    c_ref[...] = cnt_sc[...]


def _router(x, w_router_t, b_router_col, layer):
    t = x.shape[0]
    tm = min(ROUTER_TILE, t)
    return pl.pallas_call(
        functools.partial(_router_kernel, tm=tm),
        grid=(t // tm,),
        in_specs=[pl.BlockSpec((tm, D_MODEL), lambda i: (i, 0)),
                  pl.BlockSpec((None, N_EXPERTS, D_MODEL), lambda i: (layer, 0, 0)),
                  pl.BlockSpec((None, N_EXPERTS, LANES), lambda i: (layer, 0, 0))],
        out_specs=[pl.BlockSpec((TOP_K, tm), lambda i: (0, i)),
                   pl.BlockSpec((TOP_K, tm), lambda i: (0, i)),
                   pl.BlockSpec((TOP_K, tm), lambda i: (0, i)),
                   pl.BlockSpec((N_EXPERTS, LANES), lambda i: (0, 0))],
        out_shape=[jax.ShapeDtypeStruct((TOP_K, t), jnp.int32),
                   jax.ShapeDtypeStruct((TOP_K, t), F32),
                   jax.ShapeDtypeStruct((TOP_K, t), jnp.int32),
                   jax.ShapeDtypeStruct((N_EXPERTS, LANES), F32)],
        scratch_shapes=[pltpu.VMEM((N_EXPERTS, LANES), F32)],
        compiler_params=_cparams(("arbitrary",)),
        name="router",
    )(x, w_router_t, b_router_col)


SLAB_ROWS = D_MODEL // LANES
SLAB_PITCH = SLAB_ROWS + 1


def _slab_gather(idx_ref, n, src_hbm, dst, sem):
    for r in range(n):
        pltpu.make_async_copy(src_hbm.at[idx_ref[0, 0, r]], dst.at[pl.ds(r * SLAB_PITCH, SLAB_ROWS)], sem).start()


def _slab_wait(buf, n, sem):
    view = buf.at[pl.ds(0, n * SLAB_ROWS)]
    pltpu.make_async_copy(view, view, sem).wait()


def _slab_chunk(buf, first, n, c):
    return buf[pl.ds(first * SLAB_PITCH + c, n, stride=SLAB_PITCH), :]


def _expert_kernel(be_ref, nu_ref, tok_ref, x_hbm, wg_ref, wu_ref, wd_ref, o_ref,
                   xbuf, sem, wg_bf, wu_bf, wd_bf, *, blk):
    i = pl.program_id(0)
    n_used = nu_ref[0]
    slot = 1 - i % 2

    @pl.when(i < n_used)
    def _():
        _slab_gather(tok_ref, blk, x_hbm, xbuf.at[1 - slot], sem.at[1 - slot])

    @pl.when(jnp.logical_and(i >= 1, i <= n_used))
    def _():
        @pl.when(jnp.logical_or(i == 1, be_ref[i - 1] != be_ref[jnp.maximum(i - 2, 0)]))
        def _():
            wg_bf[...] = wg_ref[...].astype(BF16)
            wu_bf[...] = wu_ref[...].astype(BF16)
            wd_bf[...] = wd_ref[...].astype(BF16)

        _slab_wait(xbuf.at[slot], blk, sem.at[slot])
        xb = jnp.concatenate([_slab_chunk(xbuf.at[slot], 0, blk, c) for c in range(SLAB_ROWS)],
                             axis=1).astype(BF16)
        gate = jnp.dot(xb, wg_bf[...], preferred_element_type=F32)
        up = jnp.dot(xb, wu_bf[...], preferred_element_type=F32)
        hidden = (jax.nn.silu(gate) * up).astype(BF16)
        y = jnp.dot(hidden, wd_bf[...], preferred_element_type=F32)
        for c in range(SLAB_ROWS):
            o_ref[pl.ds(c, blk, stride=SLAB_ROWS), :] = y[:, c * LANES:(c + 1) * LANES]

    @pl.when(i > n_used)
    def _():
        o_ref[...] = jnp.zeros_like(o_ref)


def _experts(x_slabs, row_tok, block_e, n_used, w_gate, w_up, w_down, layer, blk):
    n_blocks = row_tok.shape[0]
    last = n_blocks - 1
    prev = lambda i: jnp.maximum(i - 1, 0)
    grid_spec = pltpu.PrefetchScalarGridSpec(
        num_scalar_prefetch=2,
        grid=(n_blocks + 1,),
        in_specs=[pl.BlockSpec((1, 1, blk), lambda i, be, nu: (jnp.minimum(i, last), 0, 0),
                               memory_space=pltpu.SMEM),
                  pl.BlockSpec(memory_space=pl.ANY),
                  pl.BlockSpec((None, None, D_MODEL, D_EXPERT), lambda i, be, nu: (layer, be[prev(i)], 0, 0)),
                  pl.BlockSpec((None, None, D_MODEL, D_EXPERT), lambda i, be, nu: (layer, be[prev(i)], 0, 0)),
                  pl.BlockSpec((None, None, D_EXPERT, D_MODEL), lambda i, be, nu: (layer, be[prev(i)], 0, 0))],
        out_specs=pl.BlockSpec((blk * SLAB_ROWS, LANES), lambda i, be, nu: (prev(i), 0)),
        scratch_shapes=[pltpu.VMEM((2, blk * SLAB_PITCH, LANES), F32), pltpu.SemaphoreType.DMA((2,)),
                        pltpu.VMEM((D_MODEL, D_EXPERT), BF16), pltpu.VMEM((D_MODEL, D_EXPERT), BF16),
                        pltpu.VMEM((D_EXPERT, D_MODEL), BF16)])
    y = pl.pallas_call(
        functools.partial(_expert_kernel, blk=blk),
        grid_spec=grid_spec,
        out_shape=jax.ShapeDtypeStruct((n_blocks * blk * SLAB_ROWS, LANES), F32),
        compiler_params=_cparams(("arbitrary",)),
        name="experts",
    )(block_e, n_used, row_tok, x_slabs, w_gate, w_up, w_down)
    return y.reshape(n_blocks * blk, SLAB_ROWS, LANES)


def _combine_kernel(dest_ref, x_ref, gate_ref, y_hbm, wg_ref, wu_ref, wd_ref, g_ref, b_ref, o_ref, obf_ref,
                    ybuf, sem, *, tm, n_tiles):
    i = pl.program_id(0)
    slot = 1 - i % 2

    @pl.when(i < n_tiles)
    def _():
        _slab_gather(dest_ref, TOP_K * tm, y_hbm, ybuf.at[1 - slot], sem.at[1 - slot])

    @pl.when(i >= 1)
    def _():
        x = x_ref[...]
        xb = x.astype(BF16)
        hidden = (jax.nn.silu(jnp.dot(xb, wg_ref[...], preferred_element_type=F32))
                  * jnp.dot(xb, wu_ref[...], preferred_element_type=F32)).astype(BF16)
        y = ALPHA * x + jnp.dot(hidden, wd_ref[...], preferred_element_type=F32)

        _slab_wait(ybuf.at[slot], TOP_K * tm, sem.at[slot])
        gate = gate_ref[...]
        gates = [gate[:, k:k + 1] for k in range(TOP_K)]
        routed = []
        for c in range(SLAB_ROWS):
            acc = gates[0] * _slab_chunk(ybuf.at[slot], 0, tm, c)
            for k in range(1, TOP_K):
                acc = acc + gates[k] * _slab_chunk(ybuf.at[slot], k * tm, tm, c)
            routed.append(acc)
        y = y + jnp.concatenate(routed, axis=1)
        out = _layer_norm(y, g_ref[...], b_ref[...])
        o_ref[...] = out
        obf_ref[...] = out.astype(BF16)


def _combine(x, gate_tk, dest_tiles, y_rows, wsg_bf, wsu_bf, wsd_bf, ln_g, ln_b, layer, tm):
    t = x.shape[0]
    n_tiles = t // tm
    last = n_tiles - 1
    prev = lambda i: jnp.maximum(i - 1, 0)
    vec = pl.BlockSpec((None, 1, D_MODEL), lambda i: (layer, 0, 0))
    return pl.pallas_call(
        functools.partial(_combine_kernel, tm=tm, n_tiles=n_tiles),
        grid=(n_tiles + 1,),
        in_specs=[pl.BlockSpec((1, 1, TOP_K * tm), lambda i: (jnp.minimum(i, last), 0, 0),
                               memory_space=pltpu.SMEM),
                  pl.BlockSpec((tm, D_MODEL), lambda i: (prev(i), 0)),
                  pl.BlockSpec((tm, TOP_K), lambda i: (prev(i), 0)),
                  pl.BlockSpec(memory_space=pl.ANY),
                  pl.BlockSpec((None, D_MODEL, D_EXPERT), lambda i: (layer, 0, 0)),
                  pl.BlockSpec((None, D_MODEL, D_EXPERT), lambda i: (layer, 0, 0)),
                  pl.BlockSpec((None, D_EXPERT, D_MODEL), lambda i: (layer, 0, 0)),
                  vec, vec],
        out_specs=[pl.BlockSpec((tm, D_MODEL), lambda i: (prev(i), 0)),
                   pl.BlockSpec((tm, D_MODEL), lambda i: (prev(i), 0))],
        out_shape=[jax.ShapeDtypeStruct((t, D_MODEL), F32), jax.ShapeDtypeStruct((t, D_MODEL), BF16)],
        scratch_shapes=[pltpu.VMEM((2, TOP_K * tm * SLAB_PITCH, LANES), F32), pltpu.SemaphoreType.DMA((2,))],
        compiler_params=_cparams(("arbitrary",)),
        name="combine_ln",
    )(dest_tiles, x, gate_tk, y_rows, wsg_bf, wsu_bf, wsd_bf, ln_g, ln_b)


def _dispatch_tables(top_e, rank, counts, blk, tm):
    k, t = top_e.shape
    n_blocks = (k * t) // blk + N_EXPERTS
    counts = counts[:, 0].astype(jnp.int32)
    padded = (counts + blk - 1) // blk * blk
    pend = jnp.cumsum(padded)
    pstart = pend - padded
    experts = jnp.arange(N_EXPERTS, dtype=jnp.int32)
    first_row = jnp.sum(jnp.where(top_e[None] == experts[:, None, None], pstart[:, None, None], 0), axis=0)
    dest = first_row + rank
    tok = jnp.broadcast_to(jnp.arange(t, dtype=jnp.int32)[None, :], (k, t))
    row_tok = jnp.zeros((n_blocks * blk,), jnp.int32).at[dest.reshape(-1)].set(
        tok.reshape(-1), unique_indices=True, indices_are_sorted=False)
    n_used = (pend[-1] // blk).astype(jnp.int32)
    starts = jnp.minimum(jnp.arange(n_blocks, dtype=jnp.int32), n_used - 1) * blk
    block_e = jnp.minimum(jnp.sum((pend[None, :] <= starts[:, None]).astype(jnp.int32), axis=1), N_EXPERTS - 1)
    dest_tiles = dest.reshape(k, t // tm, tm).transpose(1, 0, 2).reshape(t // tm, 1, k * tm)
    return row_tok.reshape(n_blocks, 1, blk), block_e, n_used.reshape(1), dest_tiles


def kernel(x, rel_bias, w_in, w_out, conv_w, conv_b, rg_wa, rg_ba, rg_wx, rg_bx, rg_lambda, rnn_norm_g,
           lambda_q1, lambda_k1, lambda_q2, lambda_k2, subln_g, ln1_g, ln1_b, w_router, b_router,
           w_exp_gate, w_exp_up, w_exp_down, w_sh_gate, w_sh_up, w_sh_down, ln2_g, ln2_b):
    bsz, seq, d = x.shape
    depth = w_in.shape[0]
    t = bsz * seq
    blk = min(MOE_BLOCK, t)
    ctile = min(COMBINE_TILE, t)

    w_in_bf = w_in.astype(BF16)
    w_out_bf = w_out.astype(BF16)
    wsg_bf, wsu_bf, wsd_bf = w_sh_gate.astype(BF16), w_sh_up.astype(BF16), w_sh_down.astype(BF16)
    w_router_t = jnp.swapaxes(w_router, 1, 2)
    b_router_col = jnp.broadcast_to(b_router[:, :, None], (depth, N_EXPERTS, LANES))

    def pair_blocks(w):
        w = w.reshape(depth, D_RNN // LANES, 2, RNN_BLOCK, RNN_BLOCK)
        z = jnp.zeros_like(w[:, :, 0])
        top = jnp.concatenate([w[:, :, 0], z], axis=-1)
        bot = jnp.concatenate([z, w[:, :, 1]], axis=-1)
        return jnp.concatenate([top, bot], axis=-2).astype(BF16)

    wa_bd, wx_bd = pair_blocks(rg_wa), pair_blocks(rg_wx)
    row3 = lambda a: a[:, None, :]
    lam4 = jnp.stack([lambda_q1, lambda_k1, lambda_q2, lambda_k2], axis=1)
    bias_tiles, far_bias = _bias_tables(rel_bias, min(ATT_TILE, seq))

    xt = x.reshape(t, d)
    xt_bf = xt.astype(BF16)
    for l in range(depth):
        lam_init = 0.8 - 0.6 * math.exp(-0.3 * l)
        proj = _in_proj(xt_bf, w_in_bf, l, 0, 2 * D_RNN + D_ATT, F32)
        kv = _in_proj(xt_bf, w_in_bf, l, 2 * D_RNN + D_ATT, 2 * D_ATT, BF16)
        rnn_out = _rglru(proj, bsz, seq, l, conv_w, row3(conv_b), wa_bd, row3(rg_ba), wx_bd, row3(rg_bx),
                         row3(rg_lambda), row3(rnn_norm_g))
        att_out = _attention(proj, kv, bsz, seq, l, far_bias, bias_tiles, lam4, subln_g[:, :, None], lam_init)
        xt = _out_proj_ln(xt, rnn_out, att_out, w_out_bf, row3(ln1_g), row3(ln1_b), l)
        top_e, gate, rank, counts = _router(xt, w_router_t, b_router_col, l)
        row_tok, block_e, n_used, dest_tiles = _dispatch_tables(top_e, rank, counts, blk, ctile)
        y_rows = _experts(xt.reshape(t, SLAB_ROWS, LANES), row_tok, block_e, n_used,
                          w_exp_gate, w_exp_up, w_exp_down, l, blk)
        xt, xt_bf = _combine(xt, gate.T, dest_tiles, y_rows, wsg_bf, wsu_bf, wsd_bf, row3(ln2_g), row3(ln2_b),
                             l, ctile)
    return xt.reshape(bsz, seq, d)
```

```python
import functools
import math

import jax
import jax.numpy as jnp
from jax import lax
from jax.experimental import pallas as pl
from jax.experimental.pallas import tpu as pltpu

F32 = jnp.float32
BF16 = jnp.bfloat16

D_MODEL = 2048
MODEL_DEPTH = 4
CHUNK = 64
D_RNN = 1024
RNN_BLOCK = 64
CONV_W = 4
RG_C = 8.0
D_ATT = 1024
N_HEADS = 8
HEAD_V = 128
HEAD_QK = 64
N_BUCKETS = 32
MAX_DIST = 128
N_EXPERTS = 64
TOP_K = 8
N_GROUPS = 8
GROUP_SIZE = N_EXPERTS // N_GROUPS
TOPK_GROUPS = 4
D_EXPERT = 512
ROUTED_SCALE = 2.5
ALPHA = (2.0 * MODEL_DEPTH) ** 0.25
LN_EPS = 1e-5
NEG_INF = -1e30
LOG2_E = math.log2(math.e)
W_IN_COLS = 2 * D_RNN + 3 * D_ATT
LANES = 128
SUBLANES = 8

VMEM_LIMIT = 56 * 1024 * 1024

ATT_TILE = 512
ATT_HEADS = 2
RNN_TILE = 256
MOE_BLOCK = 256
COMBINE_TILE = 128
ROUTER_TILE = 512


def _cparams(sem):
    return pltpu.CompilerParams(dimension_semantics=sem, vmem_limit_bytes=VMEM_LIMIT)


def _matmul_kernel(x_ref, w_ref, o_ref):
    o_ref[...] = jnp.dot(x_ref[...], w_ref[...], preferred_element_type=F32).astype(o_ref.dtype)


def _in_proj(x_bf, w_in_bf, layer, first_col, n, out_dtype, tn=1024):
    m, k = x_bf.shape
    tm = min(1024, m)
    j0 = first_col // tn
    return pl.pallas_call(
        _matmul_kernel,
        grid=(n // tn, m // tm),
        in_specs=[pl.BlockSpec((tm, k), lambda j, i: (i, 0)),
                  pl.BlockSpec((None, k, tn), lambda j, i: (layer, 0, j0 + j))],
        out_specs=pl.BlockSpec((tm, tn), lambda j, i: (i, j)),
        out_shape=jax.ShapeDtypeStruct((m, n), out_dtype),
        compiler_params=_cparams(("parallel", "parallel")),
        name="in_proj",
    )(x_bf, w_in_bf)


def _rglru_kernel(xr_ref, gate_ref, cw_ref, cb_ref, wa_ref, ba_ref, wx_ref, bx_ref, lam_ref, g_ref,
                  o_ref, prev_ref, h_ref, *, ts):
    @pl.when(pl.program_id(1) == 0)
    def _():
        prev_ref[...] = jnp.zeros_like(prev_ref)
        h_ref[...] = jnp.zeros_like(h_ref)

    x = xr_ref[...]
    xe = jnp.concatenate([prev_ref[...], x], axis=0)
    xc = x * cw_ref[CONV_W - 1:CONV_W, :] + cb_ref[...]
    for k in range(1, CONV_W):
        xc = xc + pltpu.roll(xe, k, axis=0)[SUBLANES:] * cw_ref[CONV_W - 1 - k:CONV_W - k, :]
    prev_ref[...] = x[ts - SUBLANES:]

    xcb = xc.astype(BF16)
    ga, gx = [], []
    for j in range(D_RNN // LANES):
        sl = xcb[:, LANES * j:LANES * (j + 1)]
        ga.append(jnp.dot(sl, wa_ref[j], preferred_element_type=F32))
        gx.append(jnp.dot(sl, wx_ref[j], preferred_element_type=F32))
    r = jax.nn.sigmoid(jnp.concatenate(ga, axis=1) + ba_ref[...])
    gi = jax.nn.sigmoid(jnp.concatenate(gx, axis=1) + bx_ref[...])
    z = -lam_ref[...]
    softplus = jnp.maximum(z, 0.0) + jnp.log1p(jnp.exp(-jnp.abs(z)))
    log_a = (-RG_C) * r * softplus
    a = jnp.exp(log_a)
    u = jnp.sqrt(-jnp.tanh(log_a) * (a * a + 1.0)) * (gi * xc)

    row = lax.broadcasted_iota(jnp.int32, (ts, D_RNN), 0)
    k = 1
    while k < SUBLANES:
        keep = row >= k
        a_sh = jnp.where(keep, pltpu.roll(a, k, axis=0), 1.0)
        u_sh = jnp.where(keep, pltpu.roll(u, k, axis=0), 0.0)
        u = a * u_sh + u
        a = a * a_sh
        k *= 2
    while k < ts:
        u = jnp.concatenate([u[:k], a[k:] * u[:ts - k] + u[k:]], axis=0)
        a = jnp.concatenate([a[:k], a[k:] * a[:ts - k]], axis=0)
        k *= 2
    h = a * h_ref[...] + u
    h_ref[...] = h[ts - 1:ts]

    ms = jnp.mean(h * h, axis=-1, keepdims=True)
    y = h * lax.rsqrt(ms + LN_EPS) * g_ref[...]
    o_ref[...] = y * jax.nn.gelu(gate_ref[...], approximate=True)


def _rglru(proj, bsz, seq, layer, conv_w, conv_b, wa_bd, rg_ba, wx_bd, rg_bx, rg_lam, rnn_g):
    ts = min(RNN_TILE, seq)
    ns = seq // ts
    row = lambda shape: pl.BlockSpec((None,) + shape, lambda b, s: (layer,) + (0,) * len(shape))
    return pl.pallas_call(
        functools.partial(_rglru_kernel, ts=ts),
        grid=(bsz, ns),
        in_specs=[pl.BlockSpec((ts, D_RNN), lambda b, s: (b * ns + s, 0)),
                  pl.BlockSpec((ts, D_RNN), lambda b, s: (b * ns + s, 1)),
                  row((CONV_W, D_RNN)), row((1, D_RNN)),
                  row((D_RNN // LANES, LANES, LANES)), row((1, D_RNN)),
                  row((D_RNN // LANES, LANES, LANES)), row((1, D_RNN)),
                  row((1, D_RNN)), row((1, D_RNN))],
        out_specs=pl.BlockSpec((ts, D_RNN), lambda b, s: (b * ns + s, 0)),
        out_shape=jax.ShapeDtypeStruct((bsz * seq, D_RNN), F32),
        scratch_shapes=[pltpu.VMEM((SUBLANES, D_RNN), F32), pltpu.VMEM((1, D_RNN), F32)],
        compiler_params=_cparams(("parallel", "arbitrary")),
        name="rglru",
    )(proj, proj, conv_w, conv_b, wa_bd, rg_ba, wx_bd, rg_bx, rg_lam, rnn_g)


def _attn_kernel(far_ref, q_ref, k_ref, v_ref, bias_ref, lam_ref, g_ref, o_ref,
                 vt_sc, m_sc, l_sc, acc_sc, *, tile, n_kv, lam_init):
    group = pl.program_id(1)
    qi = pl.program_id(2)
    heads = range(ATT_HEADS)
    cols = lambda g: slice(g * HEAD_V, (g + 1) * HEAD_V)

    @pl.when(qi == 0)
    def _():
        def prep(j, carry):
            start = pl.multiple_of(j * tile, tile)
            v = v_ref[pl.ds(start, tile), :].astype(F32)
            for g in heads:
                vt_sc[g, j] = v[:, cols(g)].T.astype(BF16)
            return carry
        lax.fori_loop(0, n_kv, prep, 0)

    row = lax.broadcasted_iota(jnp.int32, (HEAD_V, tile), 0)
    q_blk = []
    for g in heads:
        q_t = (q_ref[:, cols(g)] * (HEAD_QK ** -0.5 * LOG2_E)).T
        q_blk.append(jnp.concatenate([jnp.where(row < HEAD_QK, q_t, 0.0), jnp.where(row >= HEAD_QK, q_t, 0.0)],
                                     axis=1).astype(BF16))
    m_sc[...] = jnp.full_like(m_sc, NEG_INF)
    l_sc[...] = jnp.zeros_like(l_sc)
    acc_sc[...] = jnp.zeros_like(acc_sc)

    def block(j, biases, uniform_bias):
        start = pl.multiple_of(j * tile, tile)
        kb = k_ref[pl.ds(start, tile), :]
        for g in heads:
            bias = biases[g]
            s = jnp.dot(kb[:, cols(g)], q_blk[g], preferred_element_type=F32)
            m_old = m_sc[g]
            if uniform_bias:
                m_new = jnp.maximum(m_old, jnp.max(s, axis=0, keepdims=True) + bias)
                p = jnp.exp2(s - (m_new - bias))
            else:
                s = s + bias
                m_new = jnp.maximum(m_old, jnp.max(s, axis=0, keepdims=True))
                p = jnp.exp2(s - m_new)
            scale = jnp.exp2(m_old - m_new)
            l_sc[g] = scale * l_sc[g] + jnp.sum(p, axis=0, keepdims=True)
            acc_sc[g] = scale * acc_sc[g] + jnp.dot(vt_sc[g, j], p.astype(BF16), preferred_element_type=F32)
            m_sc[g] = m_new

    far_bias = [far_ref[group * ATT_HEADS + g] for g in heads]

    def far_body(j, carry):
        block(j, far_bias, True)
        return carry

    lax.fori_loop(0, jnp.maximum(qi - 1, 0), far_body, 0)

    @pl.when(qi >= 1)
    def _():
        block(qi - 1, [bias_ref[g, 1] for g in heads], False)

    block(qi, [bias_ref[g, 0] for g in heads], False)

    lam4 = lam_ref[...]
    lam = (jnp.exp(jnp.sum(lam4[0:1] * lam4[1:2], axis=1, keepdims=True))
           - jnp.exp(jnp.sum(lam4[2:3] * lam4[3:4], axis=1, keepdims=True)) + lam_init)
    for g in heads:
        o = acc_sc[g] / l_sc[g]
        o = o[:, :tile] - lam * o[:, tile:]
        ms = jnp.mean(o * o, axis=0, keepdims=True)
        y_t = o * lax.rsqrt(ms + LN_EPS) * g_ref[...] * (1.0 - lam_init)
        o_ref[:, cols(g)] = y_t.T


def _attention(proj, kv, bsz, seq, layer, far_bias, bias_tiles, lam4, subln_g_col, lam_init):
    tile = min(ATT_TILE, seq)
    nq = seq // tile
    width = ATT_HEADS * HEAD_V
    qcol = 2 * D_RNN // width
    vcol = D_ATT // width
    grid_spec = pltpu.PrefetchScalarGridSpec(
        num_scalar_prefetch=1,
        grid=(bsz, N_HEADS // ATT_HEADS, nq),
        in_specs=[pl.BlockSpec((tile, width), lambda b, h, i, far: (b * nq + i, qcol + h)),
                  pl.BlockSpec((seq, width), lambda b, h, i, far: (b, h)),
                  pl.BlockSpec((seq, width), lambda b, h, i, far: (b, vcol + h)),
                  pl.BlockSpec((ATT_HEADS, 2, tile, 2 * tile), lambda b, h, i, far: (h, 0, 0, 0)),
                  pl.BlockSpec((None, 4, HEAD_QK), lambda b, h, i, far: (layer, 0, 0)),
                  pl.BlockSpec((None, HEAD_V, 1), lambda b, h, i, far: (layer, 0, 0))],
        out_specs=pl.BlockSpec((tile, width), lambda b, h, i, far: (b * nq + i, h)),
        scratch_shapes=[pltpu.VMEM((ATT_HEADS, nq, HEAD_V, tile), BF16),
                        pltpu.VMEM((ATT_HEADS, 1, 2 * tile), F32), pltpu.VMEM((ATT_HEADS, 1, 2 * tile), F32),
                        pltpu.VMEM((ATT_HEADS, HEAD_V, 2 * tile), F32)])
    return pl.pallas_call(
        functools.partial(_attn_kernel, tile=tile, n_kv=nq, lam_init=lam_init),
        grid_spec=grid_spec,
        out_shape=jax.ShapeDtypeStruct((bsz * seq, D_ATT), F32),
        compiler_params=_cparams(("parallel", "parallel", "arbitrary")),
        name="diff_attn",
    )(far_bias, proj, kv, kv, bias_tiles, lam4, subln_g_col)


def _t5_bucket(rel):
    half = N_BUCKETS // 2
    max_exact = half // 2
    ret = (rel > 0).astype(jnp.int32) * half
    n = jnp.abs(rel)
    large = max_exact + (jnp.log(jnp.maximum(n, 1).astype(F32) / max_exact)
                         / math.log(MAX_DIST / max_exact) * (half - max_exact)).astype(jnp.int32)
    large = jnp.minimum(large, half - 1)
    return ret + jnp.where(n < max_exact, n, large)


def _bias_tables(rel_bias, tile):
    assert tile >= MAX_DIST and tile % CHUNK == 0
    qpos = jnp.arange(tile, dtype=jnp.int32)[:, None]
    kpos = jnp.arange(tile, dtype=jnp.int32)[None, :]

    def lookup(bucket):
        out = jnp.zeros(bucket.shape + (N_HEADS,), F32)
        for b in range(N_BUCKETS):
            out = jnp.where((bucket == b)[..., None], rel_bias[b].astype(F32), out)
        return out

    diag = lookup(_t5_bucket(kpos - qpos))
    diag = jnp.where(((kpos // CHUNK) <= (qpos // CHUNK))[..., None], diag, NEG_INF)
    prev = lookup(_t5_bucket(kpos - tile - qpos))
    tiles = jnp.transpose(jnp.stack([diag, prev], axis=0), (3, 0, 2, 1))
    tiles = jnp.concatenate([tiles, tiles], axis=-1)
    far = rel_bias[_t5_bucket(jnp.int32(-tile - 1))].astype(F32)
    return tiles * LOG2_E, far * LOG2_E


def _layer_norm(y, g, b):
    mu = jnp.mean(y, axis=-1, keepdims=True)
    d = y - mu
    var = jnp.mean(d * d, axis=-1, keepdims=True)
    return d * lax.rsqrt(var + LN_EPS) * g + b


def _out_proj_kernel(x_ref, r_ref, a_ref, w_ref, g_ref, b_ref, o_ref):
    h = jnp.dot(r_ref[...].astype(BF16), w_ref[0:D_RNN, :], preferred_element_type=F32)
    h = h + jnp.dot(a_ref[...].astype(BF16), w_ref[D_RNN:, :], preferred_element_type=F32)
    o_ref[...] = _layer_norm(ALPHA * x_ref[...] + h, g_ref[...], b_ref[...])


def _out_proj_ln(x, rnn_out, att_out, w_out_bf, ln_g, ln_b, layer, tm=256):
    t = x.shape[0]
    vec = pl.BlockSpec((None, 1, D_MODEL), lambda i: (layer, 0, 0))
    return pl.pallas_call(
        _out_proj_kernel,
        grid=(t // tm,),
        in_specs=[pl.BlockSpec((tm, D_MODEL), lambda i: (i, 0)),
                  pl.BlockSpec((tm, D_RNN), lambda i: (i, 0)),
                  pl.BlockSpec((tm, D_ATT), lambda i: (i, 0)),
                  pl.BlockSpec((None, D_MODEL, D_MODEL), lambda i: (layer, 0, 0)),
                  vec, vec],
        out_specs=pl.BlockSpec((tm, D_MODEL), lambda i: (i, 0)),
        out_shape=jax.ShapeDtypeStruct((t, D_MODEL), F32),
        compiler_params=_cparams(("parallel",)),
        name="out_proj_ln",
    )(x, rnn_out, att_out, w_out_bf, ln_g, ln_b)


def _first_index(hit, idx, size):
    return jnp.min(jnp.where(hit, idx, float(size)), axis=0, keepdims=True)


def _router_kernel(x_ref, w_ref, b_ref, e_ref, g_ref, p_ref, c_ref, cnt_sc, *, tm):
    @pl.when(pl.program_id(0) == 0)
    def _():
        cnt_sc[...] = jnp.zeros_like(cnt_sc)

    logits = lax.dot_general(w_ref[...], x_ref[...], (((1,), (1,)), ((), ())),
                             precision=lax.Precision.HIGHEST, preferred_element_type=F32)
    scores = jax.nn.sigmoid(logits)
    sel = scores + b_ref[:, 0:1]

    grp = sel.reshape(N_GROUPS, GROUP_SIZE, tm)
    gidx = lax.broadcasted_iota(jnp.int32, grp.shape, 1).astype(F32)
    m1 = jnp.max(grp, axis=1, keepdims=True)
    first = jnp.min(jnp.where(grp == m1, gidx, float(GROUP_SIZE)), axis=1, keepdims=True)
    m2 = jnp.max(jnp.where(gidx == first, -jnp.inf, grp), axis=1, keepdims=True)
    gscore = (m1 + m2).reshape(N_GROUPS, tm)

    ridx = lax.broadcasted_iota(jnp.int32, (N_GROUPS, tm), 0).astype(F32)
    gmask = jnp.zeros((N_GROUPS, tm), F32)
    for _ in range(TOPK_GROUPS):
        best = jnp.max(gscore, axis=0, keepdims=True)
        pick = ridx == _first_index(gscore == best, ridx, N_GROUPS)
        gmask = jnp.where(pick, 1.0, gmask)
        gscore = jnp.where(pick, -jnp.inf, gscore)
    emask = jnp.broadcast_to(gmask.reshape(N_GROUPS, 1, tm), (N_GROUPS, GROUP_SIZE, tm)).reshape(N_EXPERTS, tm)
    masked = jnp.where(emask > 0.0, sel, -jnp.inf)

    eidx = lax.broadcasted_iota(jnp.int32, (N_EXPERTS, tm), 0).astype(F32)
    chosen = jnp.zeros((N_EXPERTS, tm), F32)
    picks, gates = [], []
    for _ in range(TOP_K):
        best = jnp.max(masked, axis=0, keepdims=True)
        e_k = _first_index(masked == best, eidx, N_EXPERTS)
        pick = eidx == e_k
        picks.append(e_k)
        gates.append(jnp.sum(jnp.where(pick, scores, 0.0), axis=0, keepdims=True))
        chosen = jnp.where(pick, 1.0, chosen)
        masked = jnp.where(pick, -jnp.inf, masked)
    gate = jnp.concatenate(gates, axis=0)
    g_ref[...] = gate / jnp.sum(gate, axis=0, keepdims=True) * ROUTED_SCALE
    e_ref[...] = jnp.concatenate(picks, axis=0).astype(jnp.int32)

    before = (lax.broadcasted_iota(jnp.int32, (tm, tm), 0) < lax.broadcasted_iota(jnp.int32, (tm, tm), 1))
    rank = jnp.dot(chosen.astype(BF16), before.astype(BF16), preferred_element_type=F32) + cnt_sc[:, 0:1]
    p_ref[...] = jnp.concatenate(
        [jnp.sum(jnp.where(eidx == e_k, rank, 0.0), axis=0, keepdims=True) for e_k in picks],
        axis=0).astype(jnp.int32)
    cnt_sc[...] = cnt_sc[...] + jnp.sum(chosen, axis=1, keepdims=True)
    c_ref[...] = cnt_sc[...]


def _router(x, w_router_t, b_router_col, layer):
    t = x.shape[0]
    tm = min(ROUTER_TILE, t)
    return pl.pallas_call(
        functools.partial(_router_kernel, tm=tm),
        grid=(t // tm,),
        in_specs=[pl.BlockSpec((tm, D_MODEL), lambda i: (i, 0)),
                  pl.BlockSpec((None, N_EXPERTS, D_MODEL), lambda i: (layer, 0, 0)),
                  pl.BlockSpec((None, N_EXPERTS, LANES), lambda i: (layer, 0, 0))],
        out_specs=[pl.BlockSpec((TOP_K, tm), lambda i: (0, i)),
                   pl.BlockSpec((TOP_K, tm), lambda i: (0, i)),
                   pl.BlockSpec((TOP_K, tm), lambda i: (0, i)),
                   pl.BlockSpec((N_EXPERTS, LANES), lambda i: (0, 0))],
        out_shape=[jax.ShapeDtypeStruct((TOP_K, t), jnp.int32),
                   jax.ShapeDtypeStruct((TOP_K, t), F32),
                   jax.ShapeDtypeStruct((TOP_K, t), jnp.int32),
                   jax.ShapeDtypeStruct((N_EXPERTS, LANES), F32)],
        scratch_shapes=[pltpu.VMEM((N_EXPERTS, LANES), F32)],
        compiler_params=_cparams(("arbitrary",)),
        name="router",
    )(x, w_router_t, b_router_col)


SLAB_ROWS = D_MODEL // LANES
SLAB_PITCH = SLAB_ROWS + 1


def _slab_gather(idx_ref, n, src_hbm, dst, sem):
    for r in range(n):
        copy = pltpu.make_async_copy(src_hbm.at[idx_ref[0, 0, r]], dst.at[pl.ds(r * SLAB_PITCH, SLAB_ROWS)], sem)
        copy.start(priority=r % 2)


def _slab_wait(buf, n, sem):
    view = buf.at[pl.ds(0, n * SLAB_ROWS)]
    pltpu.make_async_copy(view, view, sem).wait()


def _slab_chunk(buf, first, n, c):
    return buf[pl.ds(first * SLAB_PITCH + c, n, stride=SLAB_PITCH), :]


def _expert_kernel(be_ref, nu_ref, tok_ref, x_hbm, wg_ref, wu_ref, wd_ref, o_ref,
                   xbuf, sem, wg_bf, wu_bf, wd_bf, *, blk):
    i = pl.program_id(0)
    n_used = nu_ref[0]
    slot = 1 - i % 2

    for s in range(2):
        @pl.when(jnp.logical_and(i < n_used, i % 2 == s))
        def _():
            _slab_gather(tok_ref, blk, x_hbm, xbuf.at[s], sem.at[s])

    @pl.when(jnp.logical_and(i >= 1, i <= n_used))
    def _():
        @pl.when(jnp.logical_or(i == 1, be_ref[i - 1] != be_ref[jnp.maximum(i - 2, 0)]))
        def _():
            wg_bf[...] = wg_ref[...].astype(BF16)
            wu_bf[...] = wu_ref[...].astype(BF16)
            wd_bf[...] = wd_ref[...].astype(BF16)

        _slab_wait(xbuf.at[slot], blk, sem.at[slot])
        xb = jnp.concatenate([_slab_chunk(xbuf.at[slot], 0, blk, c) for c in range(SLAB_ROWS)],
                             axis=1).astype(BF16)
        gate = jnp.dot(xb, wg_bf[...], preferred_element_type=F32)
        up = jnp.dot(xb, wu_bf[...], preferred_element_type=F32)
        hidden = (jax.nn.silu(gate) * up).astype(BF16)
        y = jnp.dot(hidden, wd_bf[...], preferred_element_type=F32)
        for c in range(SLAB_ROWS):
            o_ref[pl.ds(c, blk, stride=SLAB_ROWS), :] = y[:, c * LANES:(c + 1) * LANES]

    @pl.when(i > n_used)
    def _():
        o_ref[...] = jnp.zeros_like(o_ref)


def _experts(x_slabs, row_tok, block_e, n_used, w_gate, w_up, w_down, layer, blk):
    n_blocks = row_tok.shape[0]
    last = n_blocks - 1
    prev = lambda i: jnp.maximum(i - 1, 0)
    grid_spec = pltpu.PrefetchScalarGridSpec(
        num_scalar_prefetch=2,
        grid=(n_blocks + 1,),
        in_specs=[pl.BlockSpec((1, 1, blk), lambda i, be, nu: (jnp.minimum(i, last), 0, 0),
                               memory_space=pltpu.SMEM),
                  pl.BlockSpec(memory_space=pl.ANY),
                  pl.BlockSpec((None, None, D_MODEL, D_EXPERT), lambda i, be, nu: (layer, be[prev(i)], 0, 0)),
                  pl.BlockSpec((None, None, D_MODEL, D_EXPERT), lambda i, be, nu: (layer, be[prev(i)], 0, 0)),
                  pl.BlockSpec((None, None, D_EXPERT, D_MODEL), lambda i, be, nu: (layer, be[prev(i)], 0, 0))],
        out_specs=pl.BlockSpec((blk * SLAB_ROWS, LANES), lambda i, be, nu: (prev(i), 0)),
        scratch_shapes=[pltpu.VMEM((2, blk * SLAB_PITCH, LANES), F32), pltpu.SemaphoreType.DMA((2,)),
                        pltpu.VMEM((D_MODEL, D_EXPERT), BF16), pltpu.VMEM((D_MODEL, D_EXPERT), BF16),
                        pltpu.VMEM((D_EXPERT, D_MODEL), BF16)])
    y = pl.pallas_call(
        functools.partial(_expert_kernel, blk=blk),
        grid_spec=grid_spec,
        out_shape=jax.ShapeDtypeStruct((n_blocks * blk * SLAB_ROWS, LANES), F32),
        compiler_params=_cparams(("arbitrary",)),
        name="experts",
    )(block_e, n_used, row_tok, x_slabs, w_gate, w_up, w_down)
    return y.reshape(n_blocks * blk, SLAB_ROWS, LANES)


def _combine_kernel(dest_ref, x_ref, gate_ref, y_hbm, wg_ref, wu_ref, wd_ref, g_ref, b_ref, o_ref, obf_ref,
                    ybuf, sem, *, tm, n_tiles):
    i = pl.program_id(0)
    slot = 1 - i % 2

    for s in range(2):
        @pl.when(jnp.logical_and(i < n_tiles, i % 2 == s))
        def _():
            _slab_gather(dest_ref, TOP_K * tm, y_hbm, ybuf.at[s], sem.at[s])

    @pl.when(i >= 1)
    def _():
        x = x_ref[...]
        xb = x.astype(BF16)
        hidden = (jax.nn.silu(jnp.dot(xb, wg_ref[...], preferred_element_type=F32))
                  * jnp.dot(xb, wu_ref[...], preferred_element_type=F32)).astype(BF16)
        y = ALPHA * x + jnp.dot(hidden, wd_ref[...], preferred_element_type=F32)

        _slab_wait(ybuf.at[slot], TOP_K * tm, sem.at[slot])
        gate = gate_ref[...]
        gates = [gate[:, k:k + 1] for k in range(TOP_K)]
        routed = []
        for c in range(SLAB_ROWS):
            acc = gates[0] * _slab_chunk(ybuf.at[slot], 0, tm, c)
            for k in range(1, TOP_K):
                acc = acc + gates[k] * _slab_chunk(ybuf.at[slot], k * tm, tm, c)
            routed.append(acc)
        y = y + jnp.concatenate(routed, axis=1)
        out = _layer_norm(y, g_ref[...], b_ref[...])
        o_ref[...] = out
        obf_ref[...] = out.astype(BF16)


def _combine(x, gate_tk, dest_tiles, y_rows, wsg_bf, wsu_bf, wsd_bf, ln_g, ln_b, layer, tm):
    t = x.shape[0]
    n_tiles = t // tm
    last = n_tiles - 1
    prev = lambda i: jnp.maximum(i - 1, 0)
    vec = pl.BlockSpec((None, 1, D_MODEL), lambda i: (layer, 0, 0))
    return pl.pallas_call(
        functools.partial(_combine_kernel, tm=tm, n_tiles=n_tiles),
        grid=(n_tiles + 1,),
        in_specs=[pl.BlockSpec((1, 1, TOP_K * tm), lambda i: (jnp.minimum(i, last), 0, 0),
                               memory_space=pltpu.SMEM),
                  pl.BlockSpec((tm, D_MODEL), lambda i: (prev(i), 0)),
                  pl.BlockSpec((tm, TOP_K), lambda i: (prev(i), 0)),
                  pl.BlockSpec(memory_space=pl.ANY),
                  pl.BlockSpec((None, D_MODEL, D_EXPERT), lambda i: (layer, 0, 0)),
                  pl.BlockSpec((None, D_MODEL, D_EXPERT), lambda i: (layer, 0, 0)),
                  pl.BlockSpec((None, D_EXPERT, D_MODEL), lambda i: (layer, 0, 0)),
                  vec, vec],
        out_specs=[pl.BlockSpec((tm, D_MODEL), lambda i: (prev(i), 0)),
                   pl.BlockSpec((tm, D_MODEL), lambda i: (prev(i), 0))],
        out_shape=[jax.ShapeDtypeStruct((t, D_MODEL), F32), jax.ShapeDtypeStruct((t, D_MODEL), BF16)],
        scratch_shapes=[pltpu.VMEM((2, TOP_K * tm * SLAB_PITCH, LANES), F32), pltpu.SemaphoreType.DMA((2,))],
        compiler_params=_cparams(("arbitrary",)),
        name="combine_ln",
    )(dest_tiles, x, gate_tk, y_rows, wsg_bf, wsu_bf, wsd_bf, ln_g, ln_b)


def _dispatch_tables(top_e, rank, counts, blk, tm):
    k, t = top_e.shape
    n_blocks = (k * t) // blk + N_EXPERTS
    counts = counts[:, 0].astype(jnp.int32)
    padded = (counts + blk - 1) // blk * blk
    pend = jnp.cumsum(padded)
    pstart = pend - padded
    experts = jnp.arange(N_EXPERTS, dtype=jnp.int32)
    first_row = jnp.sum(jnp.where(top_e[None] == experts[:, None, None], pstart[:, None, None], 0), axis=0)
    dest = first_row + rank
    tok = jnp.broadcast_to(jnp.arange(t, dtype=jnp.int32)[None, :], (k, t))
    row_tok = jnp.zeros((n_blocks * blk,), jnp.int32).at[dest.reshape(-1)].set(
        tok.reshape(-1), unique_indices=True, indices_are_sorted=False)
    n_used = (pend[-1] // blk).astype(jnp.int32)
    starts = jnp.minimum(jnp.arange(n_blocks, dtype=jnp.int32), n_used - 1) * blk
    block_e = jnp.minimum(jnp.sum((pend[None, :] <= starts[:, None]).astype(jnp.int32), axis=1), N_EXPERTS - 1)
    dest_tiles = dest.reshape(k, t // tm, tm).transpose(1, 0, 2).reshape(t // tm, 1, k * tm)
    return row_tok.reshape(n_blocks, 1, blk), block_e, n_used.reshape(1), dest_tiles


def kernel(x, rel_bias, w_in, w_out, conv_w, conv_b, rg_wa, rg_ba, rg_wx, rg_bx, rg_lambda, rnn_norm_g,
           lambda_q1, lambda_k1, lambda_q2, lambda_k2, subln_g, ln1_g, ln1_b, w_router, b_router,
           w_exp_gate, w_exp_up, w_exp_down, w_sh_gate, w_sh_up, w_sh_down, ln2_g, ln2_b):
    bsz, seq, d = x.shape
    depth = w_in.shape[0]
    t = bsz * seq
    blk = min(MOE_BLOCK, t)
    ctile = min(COMBINE_TILE, t)

    w_in_bf = w_in.astype(BF16)
    w_out_bf = w_out.astype(BF16)
    wsg_bf, wsu_bf, wsd_bf = w_sh_gate.astype(BF16), w_sh_up.astype(BF16), w_sh_down.astype(BF16)
    w_router_t = jnp.swapaxes(w_router, 1, 2)
    b_router_col = jnp.broadcast_to(b_router[:, :, None], (depth, N_EXPERTS, LANES))

    def pair_blocks(w):
        w = w.reshape(depth, D_RNN // LANES, 2, RNN_BLOCK, RNN_BLOCK)
        z = jnp.zeros_like(w[:, :, 0])
        top = jnp.concatenate([w[:, :, 0], z], axis=-1)
        bot = jnp.concatenate([z, w[:, :, 1]], axis=-1)
        return jnp.concatenate([top, bot], axis=-2).astype(BF16)

    wa_bd, wx_bd = pair_blocks(rg_wa), pair_blocks(rg_wx)
    row3 = lambda a: a[:, None, :]
    lam4 = jnp.stack([lambda_q1, lambda_k1, lambda_q2, lambda_k2], axis=1)
    bias_tiles, far_bias = _bias_tables(rel_bias, min(ATT_TILE, seq))

    xt = x.reshape(t, d)
    xt_bf = xt.astype(BF16)
    for l in range(depth):
        lam_init = 0.8 - 0.6 * math.exp(-0.3 * l)
        proj = _in_proj(xt_bf, w_in_bf, l, 0, 2 * D_RNN + D_ATT, F32)
        kv = _in_proj(xt_bf, w_in_bf, l, 2 * D_RNN + D_ATT, 2 * D_ATT, BF16)
        rnn_out = _rglru(proj, bsz, seq, l, conv_w, row3(conv_b), wa_bd, row3(rg_ba), wx_bd, row3(rg_bx),
                         row3(rg_lambda), row3(rnn_norm_g))
        att_out = _attention(proj, kv, bsz, seq, l, far_bias, bias_tiles, lam4, subln_g[:, :, None], lam_init)
        xt = _out_proj_ln(xt, rnn_out, att_out, w_out_bf, row3(ln1_g), row3(ln1_b), l)
        top_e, gate, rank, counts = _router(xt, w_router_t, b_router_col, l)
        row_tok, block_e, n_used, dest_tiles = _dispatch_tables(top_e, rank, counts, blk, ctile)
        y_rows = _experts(xt.reshape(t, SLAB_ROWS, LANES), row_tok, block_e, n_used,
                          w_exp_gate, w_exp_up, w_exp_down, l, blk)
        xt, xt_bf = _combine(xt, gate.T, dest_tiles, y_rows, wsg_bf, wsu_bf, wsd_bf, row3(ln2_g), row3(ln2_b),
                             l, ctile)
    return xt.reshape(bsz, seq, d)
```

```python
import functools
import math

import jax
import jax.numpy as jnp
from jax import lax
from jax.experimental import pallas as pl
from jax.experimental.pallas import tpu as pltpu

F32 = jnp.float32
BF16 = jnp.bfloat16

D_MODEL = 2048
MODEL_DEPTH = 4
CHUNK = 64
D_RNN = 1024
RNN_BLOCK = 64
CONV_W = 4
RG_C = 8.0
D_ATT = 1024
N_HEADS = 8
HEAD_V = 128
HEAD_QK = 64
N_BUCKETS = 32
MAX_DIST = 128
N_EXPERTS = 64
TOP_K = 8
N_GROUPS = 8
GROUP_SIZE = N_EXPERTS // N_GROUPS
TOPK_GROUPS = 4
D_EXPERT = 512
ROUTED_SCALE = 2.5
ALPHA = (2.0 * MODEL_DEPTH) ** 0.25
LN_EPS = 1e-5
NEG_INF = -1e30
LOG2_E = math.log2(math.e)
W_IN_COLS = 2 * D_RNN + 3 * D_ATT
LANES = 128
SUBLANES = 8

VMEM_LIMIT = 56 * 1024 * 1024

ATT_TILE = 512
ATT_HEADS = 2
RNN_TILE = 256
MOE_BLOCK = 512
COMBINE_TILE = 256
ROUTER_TILE = 512


def _cparams(sem):
    return pltpu.CompilerParams(dimension_semantics=sem, vmem_limit_bytes=VMEM_LIMIT)


def _matmul_kernel(x_ref, w_ref, o_ref):
    o_ref[...] = jnp.dot(x_ref[...], w_ref[...], preferred_element_type=F32).astype(o_ref.dtype)


def _in_proj(x_bf, w_in_bf, layer, first_col, n, out_dtype, tn=1024):
    m, k = x_bf.shape
    tm = min(1024, m)
    j0 = first_col // tn
    return pl.pallas_call(
        _matmul_kernel,
        grid=(n // tn, m // tm),
        in_specs=[pl.BlockSpec((tm, k), lambda j, i: (i, 0)),
                  pl.BlockSpec((None, k, tn), lambda j, i: (layer, 0, j0 + j))],
        out_specs=pl.BlockSpec((tm, tn), lambda j, i: (i, j)),
        out_shape=jax.ShapeDtypeStruct((m, n), out_dtype),
        compiler_params=_cparams(("parallel", "parallel")),
        name="in_proj",
    )(x_bf, w_in_bf)


def _rglru_kernel(xr_ref, gate_ref, cw_ref, cb_ref, wa_ref, ba_ref, wx_ref, bx_ref, lam_ref, g_ref,
                  o_ref, prev_ref, h_ref, *, ts):
    @pl.when(pl.program_id(1) == 0)
    def _():
        prev_ref[...] = jnp.zeros_like(prev_ref)
        h_ref[...] = jnp.zeros_like(h_ref)

    x = xr_ref[...]
    xe = jnp.concatenate([prev_ref[...], x], axis=0)
    xc = x * cw_ref[CONV_W - 1:CONV_W, :] + cb_ref[...]
    for k in range(1, CONV_W):
        xc = xc + pltpu.roll(xe, k, axis=0)[SUBLANES:] * cw_ref[CONV_W - 1 - k:CONV_W - k, :]
    prev_ref[...] = x[ts - SUBLANES:]

    xcb = xc.astype(BF16)
    ga, gx = [], []
    for j in range(D_RNN // LANES):
        sl = xcb[:, LANES * j:LANES * (j + 1)]
        ga.append(jnp.dot(sl, wa_ref[j], preferred_element_type=F32))
        gx.append(jnp.dot(sl, wx_ref[j], preferred_element_type=F32))
    r = jax.nn.sigmoid(jnp.concatenate(ga, axis=1) + ba_ref[...])
    gi = jax.nn.sigmoid(jnp.concatenate(gx, axis=1) + bx_ref[...])
    z = -lam_ref[...]
    softplus = jnp.maximum(z, 0.0) + jnp.log1p(jnp.exp(-jnp.abs(z)))
    log_a = (-RG_C) * r * softplus
    a = jnp.exp(log_a)
    u = jnp.sqrt(-jnp.tanh(log_a) * (a * a + 1.0)) * (gi * xc)

    row = lax.broadcasted_iota(jnp.int32, (ts, D_RNN), 0)
    k = 1
    while k < SUBLANES:
        keep = row >= k
        a_sh = jnp.where(keep, pltpu.roll(a, k, axis=0), 1.0)
        u_sh = jnp.where(keep, pltpu.roll(u, k, axis=0), 0.0)
        u = a * u_sh + u
        a = a * a_sh
        k *= 2
    while k < ts:
        u = jnp.concatenate([u[:k], a[k:] * u[:ts - k] + u[k:]], axis=0)
        a = jnp.concatenate([a[:k], a[k:] * a[:ts - k]], axis=0)
        k *= 2
    h = a * h_ref[...] + u
    h_ref[...] = h[ts - 1:ts]

    ms = jnp.mean(h * h, axis=-1, keepdims=True)
    y = h * lax.rsqrt(ms + LN_EPS) * g_ref[...]
    o_ref[...] = y * jax.nn.gelu(gate_ref[...], approximate=True)


def _rglru(proj, bsz, seq, layer, conv_w, conv_b, wa_bd, rg_ba, wx_bd, rg_bx, rg_lam, rnn_g):
    ts = min(RNN_TILE, seq)
    ns = seq // ts
    row = lambda shape: pl.BlockSpec((None,) + shape, lambda b, s: (layer,) + (0,) * len(shape))
    return pl.pallas_call(
        functools.partial(_rglru_kernel, ts=ts),
        grid=(bsz, ns),
        in_specs=[pl.BlockSpec((ts, D_RNN), lambda b, s: (b * ns + s, 0)),
                  pl.BlockSpec((ts, D_RNN), lambda b, s: (b * ns + s, 1)),
                  row((CONV_W, D_RNN)), row((1, D_RNN)),
                  row((D_RNN // LANES, LANES, LANES)), row((1, D_RNN)),
                  row((D_RNN // LANES, LANES, LANES)), row((1, D_RNN)),
                  row((1, D_RNN)), row((1, D_RNN))],
        out_specs=pl.BlockSpec((ts, D_RNN), lambda b, s: (b * ns + s, 0)),
        out_shape=jax.ShapeDtypeStruct((bsz * seq, D_RNN), F32),
        scratch_shapes=[pltpu.VMEM((SUBLANES, D_RNN), F32), pltpu.VMEM((1, D_RNN), F32)],
        compiler_params=_cparams(("parallel", "arbitrary")),
        name="rglru",
    )(proj, proj, conv_w, conv_b, wa_bd, rg_ba, wx_bd, rg_bx, rg_lam, rnn_g)


def _attn_kernel(far_ref, q_ref, k_ref, v_ref, bias_ref, lam_ref, g_ref, o_ref,
                 vt_sc, m_sc, l_sc, acc_sc, *, tile, n_kv, lam_init):
    group = pl.program_id(1)
    qi = pl.program_id(2)
    heads = range(ATT_HEADS)
    cols = lambda g: slice(g * HEAD_V, (g + 1) * HEAD_V)

    @pl.when(qi == 0)
    def _():
        def prep(j, carry):
            start = pl.multiple_of(j * tile, tile)
            v = v_ref[pl.ds(start, tile), :].astype(F32)
            for g in heads:
                vt_sc[g, j] = v[:, cols(g)].T.astype(BF16)
            return carry
        lax.fori_loop(0, n_kv, prep, 0)

    row = lax.broadcasted_iota(jnp.int32, (HEAD_V, tile), 0)
    q_blk = []
    for g in heads:
        q_t = (q_ref[:, cols(g)] * (HEAD_QK ** -0.5 * LOG2_E)).T
        q_blk.append(jnp.concatenate([jnp.where(row < HEAD_QK, q_t, 0.0), jnp.where(row >= HEAD_QK, q_t, 0.0)],
                                     axis=1).astype(BF16))
    m_sc[...] = jnp.full_like(m_sc, NEG_INF)
    l_sc[...] = jnp.zeros_like(l_sc)
    acc_sc[...] = jnp.zeros_like(acc_sc)

    def block(j, biases, uniform_bias):
        start = pl.multiple_of(j * tile, tile)
        kb = k_ref[pl.ds(start, tile), :]
        for g in heads:
            bias = biases[g]
            s = jnp.dot(kb[:, cols(g)], q_blk[g], preferred_element_type=F32)
            m_old = m_sc[g]
            if uniform_bias:
                m_new = jnp.maximum(m_old, jnp.max(s, axis=0, keepdims=True) + bias)
                p = jnp.exp2(s - (m_new - bias))
            else:
                s = s + bias
                m_new = jnp.maximum(m_old, jnp.max(s, axis=0, keepdims=True))
                p = jnp.exp2(s - m_new)
            scale = jnp.exp2(m_old - m_new)
            l_sc[g] = scale * l_sc[g] + jnp.sum(p, axis=0, keepdims=True)
            acc_sc[g] = scale * acc_sc[g] + jnp.dot(vt_sc[g, j], p.astype(BF16), preferred_element_type=F32)
            m_sc[g] = m_new

    far_bias = [far_ref[group * ATT_HEADS + g] for g in heads]

    def far_body(j, carry):
        block(j, far_bias, True)
        return carry

    lax.fori_loop(0, jnp.maximum(qi - 1, 0), far_body, 0)

    @pl.when(qi >= 1)
    def _():
        block(qi - 1, [bias_ref[g, 1] for g in heads], False)

    block(qi, [bias_ref[g, 0] for g in heads], False)

    lam4 = lam_ref[...]
    lam = (jnp.exp(jnp.sum(lam4[0:1] * lam4[1:2], axis=1, keepdims=True))
           - jnp.exp(jnp.sum(lam4[2:3] * lam4[3:4], axis=1, keepdims=True)) + lam_init)
    for g in heads:
        o = acc_sc[g] / l_sc[g]
        o = o[:, :tile] - lam * o[:, tile:]
        ms = jnp.mean(o * o, axis=0, keepdims=True)
        y_t = o * lax.rsqrt(ms + LN_EPS) * g_ref[...] * (1.0 - lam_init)
        o_ref[:, cols(g)] = y_t.T


def _attention(proj, kv, bsz, seq, layer, far_bias, bias_tiles, lam4, subln_g_col, lam_init):
    tile = min(ATT_TILE, seq)
    nq = seq // tile
    width = ATT_HEADS * HEAD_V
    qcol = 2 * D_RNN // width
    vcol = D_ATT // width
    grid_spec = pltpu.PrefetchScalarGridSpec(
        num_scalar_prefetch=1,
        grid=(bsz, N_HEADS // ATT_HEADS, nq),
        in_specs=[pl.BlockSpec((tile, width), lambda b, h, i, far: (b * nq + i, qcol + h)),
                  pl.BlockSpec((seq, width), lambda b, h, i, far: (b, h)),
                  pl.BlockSpec((seq, width), lambda b, h, i, far: (b, vcol + h)),
                  pl.BlockSpec((ATT_HEADS, 2, tile, 2 * tile), lambda b, h, i, far: (h, 0, 0, 0)),
                  pl.BlockSpec((None, 4, HEAD_QK), lambda b, h, i, far: (layer, 0, 0)),
                  pl.BlockSpec((None, HEAD_V, 1), lambda b, h, i, far: (layer, 0, 0))],
        out_specs=pl.BlockSpec((tile, width), lambda b, h, i, far: (b * nq + i, h)),
        scratch_shapes=[pltpu.VMEM((ATT_HEADS, nq, HEAD_V, tile), BF16),
                        pltpu.VMEM((ATT_HEADS, 1, 2 * tile), F32), pltpu.VMEM((ATT_HEADS, 1, 2 * tile), F32),
                        pltpu.VMEM((ATT_HEADS, HEAD_V, 2 * tile), F32)])
    return pl.pallas_call(
        functools.partial(_attn_kernel, tile=tile, n_kv=nq, lam_init=lam_init),
        grid_spec=grid_spec,
        out_shape=jax.ShapeDtypeStruct((bsz * seq, D_ATT), F32),
        compiler_params=_cparams(("parallel", "parallel", "arbitrary")),
        name="diff_attn",
    )(far_bias, proj, kv, kv, bias_tiles, lam4, subln_g_col)


def _t5_bucket(rel):
    half = N_BUCKETS // 2
    max_exact = half // 2
    ret = (rel > 0).astype(jnp.int32) * half
    n = jnp.abs(rel)
    large = max_exact + (jnp.log(jnp.maximum(n, 1).astype(F32) / max_exact)
                         / math.log(MAX_DIST / max_exact) * (half - max_exact)).astype(jnp.int32)
    large = jnp.minimum(large, half - 1)
    return ret + jnp.where(n < max_exact, n, large)


def _bias_tables(rel_bias, tile):
    assert tile >= MAX_DIST and tile % CHUNK == 0
    qpos = jnp.arange(tile, dtype=jnp.int32)[:, None]
    kpos = jnp.arange(tile, dtype=jnp.int32)[None, :]

    def lookup(bucket):
        out = jnp.zeros(bucket.shape + (N_HEADS,), F32)
        for b in range(N_BUCKETS):
            out = jnp.where((bucket == b)[..., None], rel_bias[b].astype(F32), out)
        return out

    diag = lookup(_t5_bucket(kpos - qpos))
    diag = jnp.where(((kpos // CHUNK) <= (qpos // CHUNK))[..., None], diag, NEG_INF)
    prev = lookup(_t5_bucket(kpos - tile - qpos))
    tiles = jnp.transpose(jnp.stack([diag, prev], axis=0), (3, 0, 2, 1))
    tiles = jnp.concatenate([tiles, tiles], axis=-1)
    far = rel_bias[_t5_bucket(jnp.int32(-tile - 1))].astype(F32)
    return tiles * LOG2_E, far * LOG2_E


def _layer_norm(y, g, b):
    mu = jnp.mean(y, axis=-1, keepdims=True)
    d = y - mu
    var = jnp.mean(d * d, axis=-1, keepdims=True)
    return d * lax.rsqrt(var + LN_EPS) * g + b


def _out_proj_kernel(x_ref, r_ref, a_ref, w_ref, g_ref, b_ref, o_ref):
    h = jnp.dot(r_ref[...].astype(BF16), w_ref[0:D_RNN, :], preferred_element_type=F32)
    h = h + jnp.dot(a_ref[...].astype(BF16), w_ref[D_RNN:, :], preferred_element_type=F32)
    o_ref[...] = _layer_norm(ALPHA * x_ref[...] + h, g_ref[...], b_ref[...])


def _out_proj_ln(x, rnn_out, att_out, w_out_bf, ln_g, ln_b, layer, tm=256):
    t = x.shape[0]
    vec = pl.BlockSpec((None, 1, D_MODEL), lambda i: (layer, 0, 0))
    return pl.pallas_call(
        _out_proj_kernel,
        grid=(t // tm,),
        in_specs=[pl.BlockSpec((tm, D_MODEL), lambda i: (i, 0)),
                  pl.BlockSpec((tm, D_RNN), lambda i: (i, 0)),
                  pl.BlockSpec((tm, D_ATT), lambda i: (i, 0)),
                  pl.BlockSpec((None, D_MODEL, D_MODEL), lambda i: (layer, 0, 0)),
                  vec, vec],
        out_specs=pl.BlockSpec((tm, D_MODEL), lambda i: (i, 0)),
        out_shape=jax.ShapeDtypeStruct((t, D_MODEL), F32),
        compiler_params=_cparams(("parallel",)),
        name="out_proj_ln",
    )(x, rnn_out, att_out, w_out_bf, ln_g, ln_b)


def _first_index(hit, idx, size):
    return jnp.min(jnp.where(hit, idx, float(size)), axis=0, keepdims=True)


def _router_kernel(x_ref, w_ref, b_ref, e_ref, g_ref, p_ref, c_ref, cnt_sc, *, tm):
    @pl.when(pl.program_id(0) == 0)
    def _():
        cnt_sc[...] = jnp.zeros_like(cnt_sc)

    logits = lax.dot_general(w_ref[...], x_ref[...], (((1,), (1,)), ((), ())),
                             precision=lax.Precision.HIGHEST, preferred_element_type=F32)
    scores = jax.nn.sigmoid(logits)
    sel = scores + b_ref[:, 0:1]

    grp = sel.reshape(N_GROUPS, GROUP_SIZE, tm)
    gidx = lax.broadcasted_iota(jnp.int32, grp.shape, 1).astype(F32)
    m1 = jnp.max(grp, axis=1, keepdims=True)
    first = jnp.min(jnp.where(grp == m1, gidx, float(GROUP_SIZE)), axis=1, keepdims=True)
    m2 = jnp.max(jnp.where(gidx == first, -jnp.inf, grp), axis=1, keepdims=True)
    gscore = (m1 + m2).reshape(N_GROUPS, tm)

    ridx = lax.broadcasted_iota(jnp.int32, (N_GROUPS, tm), 0).astype(F32)
    gmask = jnp.zeros((N_GROUPS, tm), F32)
    for _ in range(TOPK_GROUPS):
        best = jnp.max(gscore, axis=0, keepdims=True)
        pick = ridx == _first_index(gscore == best, ridx, N_GROUPS)
        gmask = jnp.where(pick, 1.0, gmask)
        gscore = jnp.where(pick, -jnp.inf, gscore)
    emask = jnp.broadcast_to(gmask.reshape(N_GROUPS, 1, tm), (N_GROUPS, GROUP_SIZE, tm)).reshape(N_EXPERTS, tm)
    masked = jnp.where(emask > 0.0, sel, -jnp.inf)

    eidx = lax.broadcasted_iota(jnp.int32, (N_EXPERTS, tm), 0).astype(F32)
    chosen = jnp.zeros((N_EXPERTS, tm), F32)
    picks, gates = [], []
    for _ in range(TOP_K):
        best = jnp.max(masked, axis=0, keepdims=True)
        e_k = _first_index(masked == best, eidx, N_EXPERTS)
        pick = eidx == e_k
        picks.append(e_k)
        gates.append(jnp.sum(jnp.where(pick, scores, 0.0), axis=0, keepdims=True))
        chosen = jnp.where(pick, 1.0, chosen)
        masked = jnp.where(pick, -jnp.inf, masked)
    gate = jnp.concatenate(gates, axis=0)
    g_ref[...] = gate / jnp.sum(gate, axis=0, keepdims=True) * ROUTED_SCALE
    e_ref[...] = jnp.concatenate(picks, axis=0).astype(jnp.int32)

    before = (lax.broadcasted_iota(jnp.int32, (tm, tm), 0) < lax.broadcasted_iota(jnp.int32, (tm, tm), 1))
    rank = jnp.dot(chosen.astype(BF16), before.astype(BF16), preferred_element_type=F32) + cnt_sc[:, 0:1]
    p_ref[...] = jnp.concatenate(
        [jnp.sum(jnp.where(eidx == e_k, rank, 0.0), axis=0, keepdims=True) for e_k in picks],
        axis=0).astype(jnp.int32)
    cnt_sc[...] = cnt_sc[...] + jnp.sum(chosen, axis=1, keepdims=True)
    c_ref[...] = cnt_sc[...]


def _router(x, w_router_t, b_router_col, layer):
    t = x.shape[0]
    tm = min(ROUTER_TILE, t)
    return pl.pallas_call(
        functools.partial(_router_kernel, tm=tm),
        grid=(t // tm,),
        in_specs=[pl.BlockSpec((tm, D_MODEL), lambda i: (i, 0)),
                  pl.BlockSpec((None, N_EXPERTS, D_MODEL), lambda i: (layer, 0, 0)),
                  pl.BlockSpec((None, N_EXPERTS, LANES), lambda i: (layer, 0, 0))],
        out_specs=[pl.BlockSpec((TOP_K, tm), lambda i: (0, i)),
                   pl.BlockSpec((TOP_K, tm), lambda i: (0, i)),
                   pl.BlockSpec((TOP_K, tm), lambda i: (0, i)),
                   pl.BlockSpec((N_EXPERTS, LANES), lambda i: (0, 0))],
        out_shape=[jax.ShapeDtypeStruct((TOP_K, t), jnp.int32),
                   jax.ShapeDtypeStruct((TOP_K, t), F32),
                   jax.ShapeDtypeStruct((TOP_K, t), jnp.int32),
                   jax.ShapeDtypeStruct((N_EXPERTS, LANES), F32)],
        scratch_shapes=[pltpu.VMEM((N_EXPERTS, LANES), F32)],
        compiler_params=_cparams(("arbitrary",)),
        name="router",
    )(x, w_router_t, b_router_col)


HALF_MODEL = D_MODEL // 2
SLAB_ROWS = HALF_MODEL // LANES
SLAB_PITCH = SLAB_ROWS + 1
U32 = jnp.uint32


def _pack_words(lo, hi):
    lo_bits = pltpu.bitcast(lo.astype(BF16).astype(F32), U32)
    hi_bits = pltpu.bitcast(hi.astype(BF16).astype(F32), U32)
    return (lo_bits >> 16) | hi_bits


def _unpack_words(w):
    return pltpu.bitcast(w << 16, F32), pltpu.bitcast(w & U32(0xFFFF0000), F32)


def _pack_slabs(x):
    xb = x.astype(BF16)
    lo = lax.bitcast_convert_type(xb[:, :HALF_MODEL], jnp.uint16).astype(U32)
    hi = lax.bitcast_convert_type(xb[:, HALF_MODEL:], jnp.uint16).astype(U32)
    return (lo | (hi << 16)).reshape(x.shape[0], SLAB_ROWS, LANES)


def _slab_gather(idx_ref, n, src_hbm, dst, sem):
    for r in range(n):
        copy = pltpu.make_async_copy(src_hbm.at[idx_ref[0, 0, r]], dst.at[pl.ds(r * SLAB_PITCH, SLAB_ROWS)], sem)
        copy.start(priority=r % 2)


def _slab_wait(buf, n, sem):
    view = buf.at[pl.ds(0, n * SLAB_ROWS)]
    pltpu.make_async_copy(view, view, sem).wait()


def _slab_chunk(buf, first, n, s):
    return buf[pl.ds(first * SLAB_PITCH + s, n, stride=SLAB_PITCH), :]


def _expert_kernel(be_ref, nu_ref, nv_ref, tok_ref, x_hbm, wg_ref, wu_ref, wd_ref, o_ref,
                   xbuf, sem, wg_bf, wu_bf, wd_bf, *, blk):
    i = pl.program_id(0)
    n_used = nu_ref[0]
    slot = 1 - i % 2
    half = blk // 2

    for s in range(2):
        @pl.when(jnp.logical_and(i < n_used, i % 2 == s))
        def _():
            _slab_gather(tok_ref, blk, x_hbm, xbuf.at[s], sem.at[s])

    def swiglu_rows(n):
        parts = [_unpack_words(_slab_chunk(xbuf.at[slot], 0, n, s)) for s in range(SLAB_ROWS)]
        xb = jnp.concatenate([p[0] for p in parts] + [p[1] for p in parts], axis=1).astype(BF16)
        gate = jnp.dot(xb, wg_bf[...], preferred_element_type=F32)
        up = jnp.dot(xb, wu_bf[...], preferred_element_type=F32)
        hidden = (jax.nn.silu(gate) * up).astype(BF16)
        y = jnp.dot(hidden, wd_bf[...], preferred_element_type=F32)
        for s in range(SLAB_ROWS):
            words = _pack_words(y[:, s * LANES:(s + 1) * LANES],
                                y[:, HALF_MODEL + s * LANES:HALF_MODEL + (s + 1) * LANES])
            o_ref[pl.ds(s, n, stride=SLAB_ROWS), :] = words

    @pl.when(jnp.logical_and(i >= 1, i <= n_used))
    def _():
        @pl.when(jnp.logical_or(i == 1, be_ref[i - 1] != be_ref[jnp.maximum(i - 2, 0)]))
        def _():
            wg_bf[...] = wg_ref[...].astype(BF16)
            wu_bf[...] = wu_ref[...].astype(BF16)
            wd_bf[...] = wd_ref[...].astype(BF16)

        _slab_wait(xbuf.at[slot], blk, sem.at[slot])
        n_valid = nv_ref[i - 1]

        @pl.when(n_valid > half)
        def _():
            swiglu_rows(blk)

        @pl.when(n_valid <= half)
        def _():
            swiglu_rows(half)
            o_ref[pl.ds(half * SLAB_ROWS, half * SLAB_ROWS), :] = jnp.zeros((half * SLAB_ROWS, LANES), U32)

    @pl.when(i > n_used)
    def _():
        o_ref[...] = jnp.zeros_like(o_ref)


def _experts(x_slabs, row_tok, block_e, n_used, n_valid, w_gate, w_up, w_down, layer, blk):
    n_blocks = row_tok.shape[0]
    last = n_blocks - 1
    prev = lambda i: jnp.maximum(i - 1, 0)
    grid_spec = pltpu.PrefetchScalarGridSpec(
        num_scalar_prefetch=3,
        grid=(n_blocks + 1,),
        in_specs=[pl.BlockSpec((1, 1, blk), lambda i, be, nu, nv: (jnp.minimum(i, last), 0, 0),
                               memory_space=pltpu.SMEM),
                  pl.BlockSpec(memory_space=pl.ANY),
                  pl.BlockSpec((None, None, D_MODEL, D_EXPERT), lambda i, be, nu, nv: (layer, be[prev(i)], 0, 0)),
                  pl.BlockSpec((None, None, D_MODEL, D_EXPERT), lambda i, be, nu, nv: (layer, be[prev(i)], 0, 0)),
                  pl.BlockSpec((None, None, D_EXPERT, D_MODEL), lambda i, be, nu, nv: (layer, be[prev(i)], 0, 0))],
        out_specs=pl.BlockSpec((blk * SLAB_ROWS, LANES), lambda i, be, nu, nv: (prev(i), 0)),
        scratch_shapes=[pltpu.VMEM((2, blk * SLAB_PITCH, LANES), U32), pltpu.SemaphoreType.DMA((2,)),
                        pltpu.VMEM((D_MODEL, D_EXPERT), BF16), pltpu.VMEM((D_MODEL, D_EXPERT), BF16),
                        pltpu.VMEM((D_EXPERT, D_MODEL), BF16)])
    y = pl.pallas_call(
        functools.partial(_expert_kernel, blk=blk),
        grid_spec=grid_spec,
        out_shape=jax.ShapeDtypeStruct((n_blocks * blk * SLAB_ROWS, LANES), U32),
        compiler_params=_cparams(("arbitrary",)),
        name="experts",
    )(block_e, n_used, n_valid, row_tok, x_slabs, w_gate, w_up, w_down)
    return y.reshape(n_blocks * blk, SLAB_ROWS, LANES)


def _combine_kernel(dest_ref, x_ref, gate_ref, y_hbm, wg_ref, wu_ref, wd_ref, g_ref, b_ref, o_ref, obf_ref,
                    ybuf, sem, *, tm, n_tiles):
    i = pl.program_id(0)
    slot = 1 - i % 2

    for s in range(2):
        @pl.when(jnp.logical_and(i < n_tiles, i % 2 == s))
        def _():
            _slab_gather(dest_ref, TOP_K * tm, y_hbm, ybuf.at[s], sem.at[s])

    @pl.when(i >= 1)
    def _():
        x = x_ref[...]
        xb = x.astype(BF16)
        hidden = (jax.nn.silu(jnp.dot(xb, wg_ref[...], preferred_element_type=F32))
                  * jnp.dot(xb, wu_ref[...], preferred_element_type=F32)).astype(BF16)
        y = ALPHA * x + jnp.dot(hidden, wd_ref[...], preferred_element_type=F32)

        _slab_wait(ybuf.at[slot], TOP_K * tm, sem.at[slot])
        gate = gate_ref[...]
        gates = [gate[:, k:k + 1] for k in range(TOP_K)]
        routed_lo, routed_hi = [], []
        for s in range(SLAB_ROWS):
            lo, hi = _unpack_words(_slab_chunk(ybuf.at[slot], 0, tm, s))
            acc_lo, acc_hi = gates[0] * lo, gates[0] * hi
            for k in range(1, TOP_K):
                lo, hi = _unpack_words(_slab_chunk(ybuf.at[slot], k * tm, tm, s))
                acc_lo, acc_hi = acc_lo + gates[k] * lo, acc_hi + gates[k] * hi
            routed_lo.append(acc_lo)
            routed_hi.append(acc_hi)
        y = y + jnp.concatenate(routed_lo + routed_hi, axis=1)
        out = _layer_norm(y, g_ref[...], b_ref[...])
        o_ref[...] = out
        obf_ref[...] = out.astype(BF16)


def _combine(x, gate_tk, dest_tiles, y_rows, wsg_bf, wsu_bf, wsd_bf, ln_g, ln_b, layer, tm):
    t = x.shape[0]
    n_tiles = t // tm
    last = n_tiles - 1
    prev = lambda i: jnp.maximum(i - 1, 0)
    vec = pl.BlockSpec((None, 1, D_MODEL), lambda i: (layer, 0, 0))
    return pl.pallas_call(
        functools.partial(_combine_kernel, tm=tm, n_tiles=n_tiles),
        grid=(n_tiles + 1,),
        in_specs=[pl.BlockSpec((1, 1, TOP_K * tm), lambda i: (jnp.minimum(i, last), 0, 0),
                               memory_space=pltpu.SMEM),
                  pl.BlockSpec((tm, D_MODEL), lambda i: (prev(i), 0)),
                  pl.BlockSpec((tm, TOP_K), lambda i: (prev(i), 0)),
                  pl.BlockSpec(memory_space=pl.ANY),
                  pl.BlockSpec((None, D_MODEL, D_EXPERT), lambda i: (layer, 0, 0)),
                  pl.BlockSpec((None, D_MODEL, D_EXPERT), lambda i: (layer, 0, 0)),
                  pl.BlockSpec((None, D_EXPERT, D_MODEL), lambda i: (layer, 0, 0)),
                  vec, vec],
        out_specs=[pl.BlockSpec((tm, D_MODEL), lambda i: (prev(i), 0)),
                   pl.BlockSpec((tm, D_MODEL), lambda i: (prev(i), 0))],
        out_shape=[jax.ShapeDtypeStruct((t, D_MODEL), F32), jax.ShapeDtypeStruct((t, D_MODEL), BF16)],
        scratch_shapes=[pltpu.VMEM((2, TOP_K * tm * SLAB_PITCH, LANES), U32), pltpu.SemaphoreType.DMA((2,))],
        compiler_params=_cparams(("arbitrary",)),
        name="combine_ln",
    )(dest_tiles, x, gate_tk, y_rows, wsg_bf, wsu_bf, wsd_bf, ln_g, ln_b)


def _dispatch_tables(top_e, rank, counts, blk, tm):
    k, t = top_e.shape
    n_blocks = (k * t) // blk + N_EXPERTS
    counts = counts[:, 0].astype(jnp.int32)
    padded = (counts + blk - 1) // blk * blk
    pend = jnp.cumsum(padded)
    pstart = pend - padded
    experts = jnp.arange(N_EXPERTS, dtype=jnp.int32)
    first_row = jnp.sum(jnp.where(top_e[None] == experts[:, None, None], pstart[:, None, None], 0), axis=0)
    dest = first_row + rank
    tok = jnp.broadcast_to(jnp.arange(t, dtype=jnp.int32)[None, :], (k, t))
    row_tok = jnp.zeros((n_blocks * blk,), jnp.int32).at[dest.reshape(-1)].set(
        tok.reshape(-1), unique_indices=True, indices_are_sorted=False)
    n_used = (pend[-1] // blk).astype(jnp.int32)
    starts = jnp.minimum(jnp.arange(n_blocks, dtype=jnp.int32), n_used - 1) * blk
    is_e = pend[None, :] <= starts[:, None]
    block_e = jnp.minimum(jnp.sum(is_e.astype(jnp.int32), axis=1), N_EXPERTS - 1)
    mine = block_e[:, None] == experts[None, :]
    row_end = jnp.sum(jnp.where(mine, (pstart + counts)[None, :], 0), axis=1)
    n_valid = jnp.clip(row_end - starts, 0, blk).astype(jnp.int32)
    dest_tiles = dest.reshape(k, t // tm, tm).transpose(1, 0, 2).reshape(t // tm, 1, k * tm)
    return row_tok.reshape(n_blocks, 1, blk), block_e, n_used.reshape(1), n_valid, dest_tiles


def kernel(x, rel_bias, w_in, w_out, conv_w, conv_b, rg_wa, rg_ba, rg_wx, rg_bx, rg_lambda, rnn_norm_g,
           lambda_q1, lambda_k1, lambda_q2, lambda_k2, subln_g, ln1_g, ln1_b, w_router, b_router,
           w_exp_gate, w_exp_up, w_exp_down, w_sh_gate, w_sh_up, w_sh_down, ln2_g, ln2_b):
    bsz, seq, d = x.shape
    depth = w_in.shape[0]
    t = bsz * seq
    blk = min(MOE_BLOCK, t)
    ctile = min(COMBINE_TILE, t)

    w_in_bf = w_in.astype(BF16)
    w_out_bf = w_out.astype(BF16)
    wsg_bf, wsu_bf, wsd_bf = w_sh_gate.astype(BF16), w_sh_up.astype(BF16), w_sh_down.astype(BF16)
    w_router_t = jnp.swapaxes(w_router, 1, 2)
    b_router_col = jnp.broadcast_to(b_router[:, :, None], (depth, N_EXPERTS, LANES))

    def pair_blocks(w):
        w = w.reshape(depth, D_RNN // LANES, 2, RNN_BLOCK, RNN_BLOCK)
        z = jnp.zeros_like(w[:, :, 0])
        top = jnp.concatenate([w[:, :, 0], z], axis=-1)
        bot = jnp.concatenate([z, w[:, :, 1]], axis=-1)
        return jnp.concatenate([top, bot], axis=-2).astype(BF16)

    wa_bd, wx_bd = pair_blocks(rg_wa), pair_blocks(rg_wx)
    row3 = lambda a: a[:, None, :]
    lam4 = jnp.stack([lambda_q1, lambda_k1, lambda_q2, lambda_k2], axis=1)
    bias_tiles, far_bias = _bias_tables(rel_bias, min(ATT_TILE, seq))

    xt = x.reshape(t, d)
    xt_bf = xt.astype(BF16)
    for l in range(depth):
        lam_init = 0.8 - 0.6 * math.exp(-0.3 * l)
        proj = _in_proj(xt_bf, w_in_bf, l, 0, 2 * D_RNN + D_ATT, F32)
        kv = _in_proj(xt_bf, w_in_bf, l, 2 * D_RNN + D_ATT, 2 * D_ATT, BF16)
        rnn_out = _rglru(proj, bsz, seq, l, conv_w, row3(conv_b), wa_bd, row3(rg_ba), wx_bd, row3(rg_bx),
                         row3(rg_lambda), row3(rnn_norm_g))
        att_out = _attention(proj, kv, bsz, seq, l, far_bias, bias_tiles, lam4, subln_g[:, :, None], lam_init)
        xt = _out_proj_ln(xt, rnn_out, att_out, w_out_bf, row3(ln1_g), row3(ln1_b), l)
        top_e, gate, rank, counts = _router(xt, w_router_t, b_router_col, l)
        row_tok, block_e, n_used, n_valid, dest_tiles = _dispatch_tables(top_e, rank, counts, blk, ctile)
        y_rows = _experts(_pack_slabs(xt), row_tok, block_e, n_used, n_valid,
                          w_exp_gate, w_exp_up, w_exp_down, l, blk)
        xt, xt_bf = _combine(xt, gate.T, dest_tiles, y_rows, wsg_bf, wsu_bf, wsd_bf, row3(ln2_g), row3(ln2_b),
                             l, ctile)
    return xt.reshape(bsz, seq, d)
```

```python
import functools
import math

import jax
import jax.numpy as jnp
from jax import lax
from jax.experimental import pallas as pl
from jax.experimental.pallas import tpu as pltpu

F32 = jnp.float32
BF16 = jnp.bfloat16

D_MODEL = 2048
MODEL_DEPTH = 4
CHUNK = 64
D_RNN = 1024
RNN_BLOCK = 64
CONV_W = 4
RG_C = 8.0
D_ATT = 1024
N_HEADS = 8
HEAD_V = 128
HEAD_QK = 64
N_BUCKETS = 32
MAX_DIST = 128
N_EXPERTS = 64
TOP_K = 8
N_GROUPS = 8
GROUP_SIZE = N_EXPERTS // N_GROUPS
TOPK_GROUPS = 4
D_EXPERT = 512
ROUTED_SCALE = 2.5
ALPHA = (2.0 * MODEL_DEPTH) ** 0.25
LN_EPS = 1e-5
NEG_INF = -1e30
LOG2_E = math.log2(math.e)
W_IN_COLS = 2 * D_RNN + 3 * D_ATT
LANES = 128
SUBLANES = 8

VMEM_LIMIT = 56 * 1024 * 1024

ATT_TILE = 512
ATT_HEADS = 2
RNN_TILE = 256
MOE_BLOCK = 256
COMBINE_TILE = 128
ROUTER_TILE = 512


def _cparams(sem):
    return pltpu.CompilerParams(dimension_semantics=sem, vmem_limit_bytes=VMEM_LIMIT)


def _matmul_kernel(x_ref, w_ref, o_ref):
    o_ref[...] = jnp.dot(x_ref[...], w_ref[...], preferred_element_type=F32).astype(o_ref.dtype)


def _in_proj(x_bf, w_in_bf, layer, first_col, n, out_dtype, tn=1024):
    m, k = x_bf.shape
    tm = min(1024, m)
    j0 = first_col // tn
    return pl.pallas_call(
        _matmul_kernel,
        grid=(n // tn, m // tm),
        in_specs=[pl.BlockSpec((tm, k), lambda j, i: (i, 0)),
                  pl.BlockSpec((None, k, tn), lambda j, i: (layer, 0, j0 + j))],
        out_specs=pl.BlockSpec((tm, tn), lambda j, i: (i, j)),
        out_shape=jax.ShapeDtypeStruct((m, n), out_dtype),
        compiler_params=_cparams(("parallel", "parallel")),
        name="in_proj",
    )(x_bf, w_in_bf)


def _rglru_kernel(xr_ref, gate_ref, cw_ref, cb_ref, wa_ref, ba_ref, wx_ref, bx_ref, lam_ref, g_ref,
                  o_ref, prev_ref, h_ref, *, ts):
    @pl.when(pl.program_id(1) == 0)
    def _():
        prev_ref[...] = jnp.zeros_like(prev_ref)
        h_ref[...] = jnp.zeros_like(h_ref)

    x = xr_ref[...]
    xe = jnp.concatenate([prev_ref[...], x], axis=0)
    xc = x * cw_ref[CONV_W - 1:CONV_W, :] + cb_ref[...]
    for k in range(1, CONV_W):
        xc = xc + pltpu.roll(xe, k, axis=0)[SUBLANES:] * cw_ref[CONV_W - 1 - k:CONV_W - k, :]
    prev_ref[...] = x[ts - SUBLANES:]

    xcb = xc.astype(BF16)
    ga, gx = [], []
    for j in range(D_RNN // LANES):
        sl = xcb[:, LANES * j:LANES * (j + 1)]
        ga.append(jnp.dot(sl, wa_ref[j], preferred_element_type=F32))
        gx.append(jnp.dot(sl, wx_ref[j], preferred_element_type=F32))
    r = jax.nn.sigmoid(jnp.concatenate(ga, axis=1) + ba_ref[...])
    gi = jax.nn.sigmoid(jnp.concatenate(gx, axis=1) + bx_ref[...])
    z = -lam_ref[...]
    softplus = jnp.maximum(z, 0.0) + jnp.log1p(jnp.exp(-jnp.abs(z)))
    log_a = (-RG_C) * r * softplus
    a = jnp.exp(log_a)
    u = jnp.sqrt(-jnp.tanh(log_a) * (a * a + 1.0)) * (gi * xc)

    row = lax.broadcasted_iota(jnp.int32, (ts, D_RNN), 0)
    k = 1
    while k < SUBLANES:
        keep = row >= k
        a_sh = jnp.where(keep, pltpu.roll(a, k, axis=0), 1.0)
        u_sh = jnp.where(keep, pltpu.roll(u, k, axis=0), 0.0)
        u = a * u_sh + u
        a = a * a_sh
        k *= 2
    while k < ts:
        u = jnp.concatenate([u[:k], a[k:] * u[:ts - k] + u[k:]], axis=0)
        a = jnp.concatenate([a[:k], a[k:] * a[:ts - k]], axis=0)
        k *= 2
    h = a * h_ref[...] + u
    h_ref[...] = h[ts - 1:ts]

    ms = jnp.mean(h * h, axis=-1, keepdims=True)
    y = h * lax.rsqrt(ms + LN_EPS) * g_ref[...]
    o_ref[...] = y * jax.nn.gelu(gate_ref[...], approximate=True)


def _rglru(proj, bsz, seq, layer, conv_w, conv_b, wa_bd, rg_ba, wx_bd, rg_bx, rg_lam, rnn_g):
    ts = min(RNN_TILE, seq)
    ns = seq // ts
    row = lambda shape: pl.BlockSpec((None,) + shape, lambda b, s: (layer,) + (0,) * len(shape))
    return pl.pallas_call(
        functools.partial(_rglru_kernel, ts=ts),
        grid=(bsz, ns),
        in_specs=[pl.BlockSpec((ts, D_RNN), lambda b, s: (b * ns + s, 0)),
                  pl.BlockSpec((ts, D_RNN), lambda b, s: (b * ns + s, 1)),
                  row((CONV_W, D_RNN)), row((1, D_RNN)),
                  row((D_RNN // LANES, LANES, LANES)), row((1, D_RNN)),
                  row((D_RNN // LANES, LANES, LANES)), row((1, D_RNN)),
                  row((1, D_RNN)), row((1, D_RNN))],
        out_specs=pl.BlockSpec((ts, D_RNN), lambda b, s: (b * ns + s, 0)),
        out_shape=jax.ShapeDtypeStruct((bsz * seq, D_RNN), F32),
        scratch_shapes=[pltpu.VMEM((SUBLANES, D_RNN), F32), pltpu.VMEM((1, D_RNN), F32)],
        compiler_params=_cparams(("parallel", "arbitrary")),
        name="rglru",
    )(proj, proj, conv_w, conv_b, wa_bd, rg_ba, wx_bd, rg_bx, rg_lam, rnn_g)


def _attn_kernel(far_ref, q_ref, k_ref, v_ref, bias_ref, lam_ref, g_ref, o_ref,
                 vt_sc, s_sc, m_sc, l_sc, acc_sc, *, tile, n_kv, lam_init):
    group = pl.program_id(1)
    qi = pl.program_id(2)
    heads = range(ATT_HEADS)
    cols = lambda g: slice(g * HEAD_V, (g + 1) * HEAD_V)

    @pl.when(qi == 0)
    def _():
        def prep(j, carry):
            start = pl.multiple_of(j * tile, tile)
            v = v_ref[pl.ds(start, tile), :].astype(F32)
            for g in heads:
                vt_sc[g, j] = v[:, cols(g)].T.astype(BF16)
            return carry
        lax.fori_loop(0, n_kv, prep, 0)

    row = lax.broadcasted_iota(jnp.int32, (HEAD_V, tile), 0)
    q_blk = []
    for g in heads:
        q_t = (q_ref[:, cols(g)] * (HEAD_QK ** -0.5 * LOG2_E)).T
        q_blk.append(jnp.concatenate([jnp.where(row < HEAD_QK, q_t, 0.0), jnp.where(row >= HEAD_QK, q_t, 0.0)],
                                     axis=1).astype(BF16))
    m_sc[...] = jnp.full_like(m_sc, NEG_INF)
    l_sc[...] = jnp.zeros_like(l_sc)
    acc_sc[...] = jnp.zeros_like(acc_sc)

    def logits(j, buf):
        start = pl.multiple_of(j * tile, tile)
        kb = k_ref[pl.ds(start, tile), :]
        for g in heads:
            s_sc[buf, g] = jnp.dot(kb[:, cols(g)], q_blk[g], preferred_element_type=F32)

    def block(j, buf, biases, uniform_bias):
        for g in heads:
            bias = biases[g]
            s = s_sc[buf, g]
            m_old = m_sc[g]
            if uniform_bias:
                m_new = jnp.maximum(m_old, jnp.max(s, axis=0, keepdims=True) + bias)
                p = jnp.exp2(s - (m_new - bias))
            else:
                s = s + bias
                m_new = jnp.maximum(m_old, jnp.max(s, axis=0, keepdims=True))
                p = jnp.exp2(s - m_new)
            scale = jnp.exp2(m_old - m_new)
            l_sc[g] = scale * l_sc[g] + jnp.sum(p, axis=0, keepdims=True)
            acc_sc[g] = scale * acc_sc[g] + jnp.dot(vt_sc[g, j], p.astype(BF16), preferred_element_type=F32)
            m_sc[g] = m_new

    far_bias = [far_ref[group * ATT_HEADS + g] for g in heads]

    n_far = jnp.maximum(qi - 1, 0)
    diag_bias = lambda: [bias_ref[g, 0] for g in heads]
    prev_bias = lambda: [bias_ref[g, 1] for g in heads]
    logits(0, 0)

    def far_pair(p, carry):
        logits(2 * p + 1, 1)
        block(2 * p, 0, far_bias, True)
        logits(2 * p + 2, 0)
        block(2 * p + 1, 1, far_bias, True)
        return carry

    lax.fori_loop(0, n_far // 2, far_pair, 0)

    @pl.when(qi == 0)
    def _():
        block(0, 0, diag_bias(), False)

    @pl.when(jnp.logical_and(qi >= 1, n_far % 2 == 0))
    def _():
        logits(qi, 1)
        block(qi - 1, 0, prev_bias(), False)
        block(qi, 1, diag_bias(), False)

    @pl.when(n_far % 2 == 1)
    def _():
        logits(qi - 1, 1)
        block(qi - 2, 0, far_bias, True)
        logits(qi, 0)
        block(qi - 1, 1, prev_bias(), False)
        block(qi, 0, diag_bias(), False)

    lam4 = lam_ref[...]
    lam = (jnp.exp(jnp.sum(lam4[0:1] * lam4[1:2], axis=1, keepdims=True))
           - jnp.exp(jnp.sum(lam4[2:3] * lam4[3:4], axis=1, keepdims=True)) + lam_init)
    for g in heads:
        o = acc_sc[g] / l_sc[g]
        o = o[:, :tile] - lam * o[:, tile:]
        ms = jnp.mean(o * o, axis=0, keepdims=True)
        y_t = o * lax.rsqrt(ms + LN_EPS) * g_ref[...] * (1.0 - lam_init)
        o_ref[:, cols(g)] = y_t.T


def _attention(proj, kv, bsz, seq, layer, far_bias, bias_tiles, lam4, subln_g_col, lam_init):
    tile = min(ATT_TILE, seq)
    nq = seq // tile
    width = ATT_HEADS * HEAD_V
    qcol = 2 * D_RNN // width
    vcol = D_ATT // width
    grid_spec = pltpu.PrefetchScalarGridSpec(
        num_scalar_prefetch=1,
        grid=(bsz, N_HEADS // ATT_HEADS, nq),
        in_specs=[pl.BlockSpec((tile, width), lambda b, h, i, far: (b * nq + i, qcol + h)),
                  pl.BlockSpec((seq, width), lambda b, h, i, far: (b, h)),
                  pl.BlockSpec((seq, width), lambda b, h, i, far: (b, vcol + h)),
                  pl.BlockSpec((ATT_HEADS, 2, tile, 2 * tile), lambda b, h, i, far: (h, 0, 0, 0)),
                  pl.BlockSpec((None, 4, HEAD_QK), lambda b, h, i, far: (layer, 0, 0)),
                  pl.BlockSpec((None, HEAD_V, 1), lambda b, h, i, far: (layer, 0, 0))],
        out_specs=pl.BlockSpec((tile, width), lambda b, h, i, far: (b * nq + i, h)),
        scratch_shapes=[pltpu.VMEM((ATT_HEADS, nq, HEAD_V, tile), BF16),
                        pltpu.VMEM((2, ATT_HEADS, tile, 2 * tile), F32),
                        pltpu.VMEM((ATT_HEADS, 1, 2 * tile), F32), pltpu.VMEM((ATT_HEADS, 1, 2 * tile), F32),
                        pltpu.VMEM((ATT_HEADS, HEAD_V, 2 * tile), F32)])
    return pl.pallas_call(
        functools.partial(_attn_kernel, tile=tile, n_kv=nq, lam_init=lam_init),
        grid_spec=grid_spec,
        out_shape=jax.ShapeDtypeStruct((bsz * seq, D_ATT), F32),
        compiler_params=_cparams(("parallel", "parallel", "arbitrary")),
        name="diff_attn",
    )(far_bias, proj, kv, kv, bias_tiles, lam4, subln_g_col)


def _t5_bucket(rel):
    half = N_BUCKETS // 2
    max_exact = half // 2
    ret = (rel > 0).astype(jnp.int32) * half
    n = jnp.abs(rel)
    large = max_exact + (jnp.log(jnp.maximum(n, 1).astype(F32) / max_exact)
                         / math.log(MAX_DIST / max_exact) * (half - max_exact)).astype(jnp.int32)
    large = jnp.minimum(large, half - 1)
    return ret + jnp.where(n < max_exact, n, large)


def _bias_tables(rel_bias, tile):
    assert tile >= MAX_DIST and tile % CHUNK == 0
    qpos = jnp.arange(tile, dtype=jnp.int32)[:, None]
    kpos = jnp.arange(tile, dtype=jnp.int32)[None, :]

    def lookup(bucket):
        out = jnp.zeros(bucket.shape + (N_HEADS,), F32)
        for b in range(N_BUCKETS):
            out = jnp.where((bucket == b)[..., None], rel_bias[b].astype(F32), out)
        return out

    diag = lookup(_t5_bucket(kpos - qpos))
    diag = jnp.where(((kpos // CHUNK) <= (qpos // CHUNK))[..., None], diag, NEG_INF)
    prev = lookup(_t5_bucket(kpos - tile - qpos))
    tiles = jnp.transpose(jnp.stack([diag, prev], axis=0), (3, 0, 2, 1))
    tiles = jnp.concatenate([tiles, tiles], axis=-1)
    far = rel_bias[_t5_bucket(jnp.int32(-tile - 1))].astype(F32)
    return tiles * LOG2_E, far * LOG2_E


def _layer_norm(y, g, b):
    mu = jnp.mean(y, axis=-1, keepdims=True)
    d = y - mu
    var = jnp.mean(d * d, axis=-1, keepdims=True)
    return d * lax.rsqrt(var + LN_EPS) * g + b


def _out_proj_kernel(x_ref, r_ref, a_ref, w_ref, g_ref, b_ref, o_ref):
    h = jnp.dot(r_ref[...].astype(BF16), w_ref[0:D_RNN, :], preferred_element_type=F32)
    h = h + jnp.dot(a_ref[...].astype(BF16), w_ref[D_RNN:, :], preferred_element_type=F32)
    o_ref[...] = _layer_norm(ALPHA * x_ref[...] + h, g_ref[...], b_ref[...])


def _out_proj_ln(x, rnn_out, att_out, w_out_bf, ln_g, ln_b, layer, tm=256):
    t = x.shape[0]
    vec = pl.BlockSpec((None, 1, D_MODEL), lambda i: (layer, 0, 0))
    return pl.pallas_call(
        _out_proj_kernel,
        grid=(t // tm,),
        in_specs=[pl.BlockSpec((tm, D_MODEL), lambda i: (i, 0)),
                  pl.BlockSpec((tm, D_RNN), lambda i: (i, 0)),
                  pl.BlockSpec((tm, D_ATT), lambda i: (i, 0)),
                  pl.BlockSpec((None, D_MODEL, D_MODEL), lambda i: (layer, 0, 0)),
                  vec, vec],
        out_specs=pl.BlockSpec((tm, D_MODEL), lambda i: (i, 0)),
        out_shape=jax.ShapeDtypeStruct((t, D_MODEL), F32),
        compiler_params=_cparams(("parallel",)),
        name="out_proj_ln",
    )(x, rnn_out, att_out, w_out_bf, ln_g, ln_b)


def _first_index(hit, idx, size):
    return jnp.min(jnp.where(hit, idx, float(size)), axis=0, keepdims=True)


def _router_kernel(x_ref, w_ref, b_ref, e_ref, g_ref, p_ref, c_ref, cnt_sc, *, tm):
    @pl.when(pl.program_id(0) == 0)
    def _():
        cnt_sc[...] = jnp.zeros_like(cnt_sc)

    logits = lax.dot_general(w_ref[...], x_ref[...], (((1,), (1,)), ((), ())),
                             precision=lax.Precision.HIGHEST, preferred_element_type=F32)
    scores = jax.nn.sigmoid(logits)
    sel = scores + b_ref[:, 0:1]

    grp = sel.reshape(N_GROUPS, GROUP_SIZE, tm)
    gidx = lax.broadcasted_iota(jnp.int32, grp.shape, 1).astype(F32)
    m1 = jnp.max(grp, axis=1, keepdims=True)
    first = jnp.min(jnp.where(grp == m1, gidx, float(GROUP_SIZE)), axis=1, keepdims=True)
    m2 = jnp.max(jnp.where(gidx == first, -jnp.inf, grp), axis=1, keepdims=True)
    gscore = (m1 + m2).reshape(N_GROUPS, tm)

    ridx = lax.broadcasted_iota(jnp.int32, (N_GROUPS, tm), 0).astype(F32)
    gmask = jnp.zeros((N_GROUPS, tm), F32)
    for _ in range(TOPK_GROUPS):
        best = jnp.max(gscore, axis=0, keepdims=True)
        pick = ridx == _first_index(gscore == best, ridx, N_GROUPS)
        gmask = jnp.where(pick, 1.0, gmask)
        gscore = jnp.where(pick, -jnp.inf, gscore)
    emask = jnp.broadcast_to(gmask.reshape(N_GROUPS, 1, tm), (N_GROUPS, GROUP_SIZE, tm)).reshape(N_EXPERTS, tm)
    masked = jnp.where(emask > 0.0, sel, -jnp.inf)

    eidx = lax.broadcasted_iota(jnp.int32, (N_EXPERTS, tm), 0).astype(F32)
    chosen = jnp.zeros((N_EXPERTS, tm), F32)
    picks, gates = [], []
    for _ in range(TOP_K):
        best = jnp.max(masked, axis=0, keepdims=True)
        e_k = _first_index(masked == best, eidx, N_EXPERTS)
        pick = eidx == e_k
        picks.append(e_k)
        gates.append(jnp.sum(jnp.where(pick, scores, 0.0), axis=0, keepdims=True))
        chosen = jnp.where(pick, 1.0, chosen)
        masked = jnp.where(pick, -jnp.inf, masked)
    gate = jnp.concatenate(gates, axis=0)
    g_ref[...] = gate / jnp.sum(gate, axis=0, keepdims=True) * ROUTED_SCALE
    e_ref[...] = jnp.concatenate(picks, axis=0).astype(jnp.int32)

    before = (lax.broadcasted_iota(jnp.int32, (tm, tm), 0) < lax.broadcasted_iota(jnp.int32, (tm, tm), 1))
    rank = jnp.dot(chosen.astype(BF16), before.astype(BF16), preferred_element_type=F32) + cnt_sc[:, 0:1]
    p_ref[...] = jnp.concatenate(
        [jnp.sum(jnp.where(eidx == e_k, rank, 0.0), axis=0, keepdims=True) for e_k in picks],
        axis=0).astype(jnp.int32)
    cnt_sc[...] = cnt_sc[...] + jnp.sum(chosen, axis=1, keepdims=True)
    c_ref[...] = cnt_sc[...]


def _router(x, w_router_t, b_router_col, layer):
    t = x.shape[0]
    tm = min(ROUTER_TILE, t)
    return pl.pallas_call(
        functools.partial(_router_kernel, tm=tm),
        grid=(t // tm,),
        in_specs=[pl.BlockSpec((tm, D_MODEL), lambda i: (i, 0)),
                  pl.BlockSpec((None, N_EXPERTS, D_MODEL), lambda i: (layer, 0, 0)),
                  pl.BlockSpec((None, N_EXPERTS, LANES), lambda i: (layer, 0, 0))],
        out_specs=[pl.BlockSpec((TOP_K, tm), lambda i: (0, i)),
                   pl.BlockSpec((TOP_K, tm), lambda i: (0, i)),
                   pl.BlockSpec((TOP_K, tm), lambda i: (0, i)),
                   pl.BlockSpec((N_EXPERTS, LANES), lambda i: (0, 0))],
        out_shape=[jax.ShapeDtypeStruct((TOP_K, t), jnp.int32),
                   jax.ShapeDtypeStruct((TOP_K, t), F32),
                   jax.ShapeDtypeStruct((TOP_K, t), jnp.int32),
                   jax.ShapeDtypeStruct((N_EXPERTS, LANES), F32)],
        scratch_shapes=[pltpu.VMEM((N_EXPERTS, LANES), F32)],
        compiler_params=_cparams(("arbitrary",)),
        name="router",
    )(x, w_router_t, b_router_col)


SLAB_ROWS = D_MODEL // LANES
SLAB_PITCH = SLAB_ROWS + 1


def _slab_gather(idx_ref, n, src_hbm, dst, sem):
    for r in range(n):
        copy = pltpu.make_async_copy(src_hbm.at[idx_ref[0, 0, r]], dst.at[pl.ds(r * SLAB_PITCH, SLAB_ROWS)], sem)
        copy.start(priority=r % 2)


def _slab_wait(buf, n, sem):
    view = buf.at[pl.ds(0, n * SLAB_ROWS)]
    pltpu.make_async_copy(view, view, sem).wait()


def _slab_chunk(buf, first, n, c):
    return buf[pl.ds(first * SLAB_PITCH + c, n, stride=SLAB_PITCH), :]


def _expert_kernel(be_ref, nu_ref, tok_ref, x_hbm, wg_ref, wu_ref, wd_ref, o_ref,
                   xbuf, sem, wg_bf, wu_bf, wd_bf, *, blk):
    i = pl.program_id(0)
    n_used = nu_ref[0]
    slot = 1 - i % 2

    for s in range(2):
        @pl.when(jnp.logical_and(i < n_used, i % 2 == s))
        def _():
            _slab_gather(tok_ref, blk, x_hbm, xbuf.at[s], sem.at[s])

    @pl.when(jnp.logical_and(i >= 1, i <= n_used))
    def _():
        @pl.when(jnp.logical_or(i == 1, be_ref[i - 1] != be_ref[jnp.maximum(i - 2, 0)]))
        def _():
            wg_bf[...] = wg_ref[...].astype(BF16)
            wu_bf[...] = wu_ref[...].astype(BF16)
            wd_bf[...] = wd_ref[...].astype(BF16)

        _slab_wait(xbuf.at[slot], blk, sem.at[slot])
        xb = jnp.concatenate([_slab_chunk(xbuf.at[slot], 0, blk, c) for c in range(SLAB_ROWS)],
                             axis=1).astype(BF16)
        gate = jnp.dot(xb, wg_bf[...], preferred_element_type=F32)
        up = jnp.dot(xb, wu_bf[...], preferred_element_type=F32)
        hidden = (jax.nn.silu(gate) * up).astype(BF16)
        y = jnp.dot(hidden, wd_bf[...], preferred_element_type=F32)
        for c in range(SLAB_ROWS):
            o_ref[pl.ds(c, blk, stride=SLAB_ROWS), :] = y[:, c * LANES:(c + 1) * LANES]

    @pl.when(i > n_used)
    def _():
        o_ref[...] = jnp.zeros_like(o_ref)


def _experts(x_slabs, row_tok, block_e, n_used, w_gate, w_up, w_down, layer, blk):
    n_blocks = row_tok.shape[0]
    last = n_blocks - 1
    prev = lambda i: jnp.maximum(i - 1, 0)
    grid_spec = pltpu.PrefetchScalarGridSpec(
        num_scalar_prefetch=2,
        grid=(n_blocks + 1,),
        in_specs=[pl.BlockSpec((1, 1, blk), lambda i, be, nu: (jnp.minimum(i, last), 0, 0),
                               memory_space=pltpu.SMEM),
                  pl.BlockSpec(memory_space=pl.ANY),
                  pl.BlockSpec((None, None, D_MODEL, D_EXPERT), lambda i, be, nu: (layer, be[prev(i)], 0, 0)),
                  pl.BlockSpec((None, None, D_MODEL, D_EXPERT), lambda i, be, nu: (layer, be[prev(i)], 0, 0)),
                  pl.BlockSpec((None, None, D_EXPERT, D_MODEL), lambda i, be, nu: (layer, be[prev(i)], 0, 0))],
        out_specs=pl.BlockSpec((blk * SLAB_ROWS, LANES), lambda i, be, nu: (prev(i), 0)),
        scratch_shapes=[pltpu.VMEM((2, blk * SLAB_PITCH, LANES), F32), pltpu.SemaphoreType.DMA((2,)),
                        pltpu.VMEM((D_MODEL, D_EXPERT), BF16), pltpu.VMEM((D_MODEL, D_EXPERT), BF16),
                        pltpu.VMEM((D_EXPERT, D_MODEL), BF16)])
    y = pl.pallas_call(
        functools.partial(_expert_kernel, blk=blk),
        grid_spec=grid_spec,
        out_shape=jax.ShapeDtypeStruct((n_blocks * blk * SLAB_ROWS, LANES), F32),
        compiler_params=_cparams(("arbitrary",)),
        name="experts",
    )(block_e, n_used, row_tok, x_slabs, w_gate, w_up, w_down)
    return y.reshape(n_blocks * blk, SLAB_ROWS, LANES)


def _combine_kernel(dest_ref, x_ref, gate_ref, y_hbm, wg_ref, wu_ref, wd_ref, g_ref, b_ref, o_ref, obf_ref,
                    ybuf, sem, *, tm, n_tiles):
    i = pl.program_id(0)
    slot = 1 - i % 2

    for s in range(2):
        @pl.when(jnp.logical_and(i < n_tiles, i % 2 == s))
        def _():
            _slab_gather(dest_ref, TOP_K * tm, y_hbm, ybuf.at[s], sem.at[s])

    @pl.when(i >= 1)
    def _():
        x = x_ref[...]
        xb = x.astype(BF16)
        hidden = (jax.nn.silu(jnp.dot(xb, wg_ref[...], preferred_element_type=F32))
                  * jnp.dot(xb, wu_ref[...], preferred_element_type=F32)).astype(BF16)
        y = ALPHA * x + jnp.dot(hidden, wd_ref[...], preferred_element_type=F32)

        _slab_wait(ybuf.at[slot], TOP_K * tm, sem.at[slot])
        gate = gate_ref[...]
        gates = [gate[:, k:k + 1] for k in range(TOP_K)]
        routed = []
        for c in range(SLAB_ROWS):
            acc = gates[0] * _slab_chunk(ybuf.at[slot], 0, tm, c)
            for k in range(1, TOP_K):
                acc = acc + gates[k] * _slab_chunk(ybuf.at[slot], k * tm, tm, c)
            routed.append(acc)
        y = y + jnp.concatenate(routed, axis=1)
        out = _layer_norm(y, g_ref[...], b_ref[...])
        o_ref[...] = out
        obf_ref[...] = out.astype(BF16)


def _combine(x, gate_tk, dest_tiles, y_rows, wsg_bf, wsu_bf, wsd_bf, ln_g, ln_b, layer, tm):
    t = x.shape[0]
    n_tiles = t // tm
    last = n_tiles - 1
    prev = lambda i: jnp.maximum(i - 1, 0)
    vec = pl.BlockSpec((None, 1, D_MODEL), lambda i: (layer, 0, 0))
    return pl.pallas_call(
        functools.partial(_combine_kernel, tm=tm, n_tiles=n_tiles),
        grid=(n_tiles + 1,),
        in_specs=[pl.BlockSpec((1, 1, TOP_K * tm), lambda i: (jnp.minimum(i, last), 0, 0),
                               memory_space=pltpu.SMEM),
                  pl.BlockSpec((tm, D_MODEL), lambda i: (prev(i), 0)),
                  pl.BlockSpec((tm, TOP_K), lambda i: (prev(i), 0)),
                  pl.BlockSpec(memory_space=pl.ANY),
                  pl.BlockSpec((None, D_MODEL, D_EXPERT), lambda i: (layer, 0, 0)),
                  pl.BlockSpec((None, D_MODEL, D_EXPERT), lambda i: (layer, 0, 0)),
                  pl.BlockSpec((None, D_EXPERT, D_MODEL), lambda i: (layer, 0, 0)),
                  vec, vec],
        out_specs=[pl.BlockSpec((tm, D_MODEL), lambda i: (prev(i), 0)),
                   pl.BlockSpec((tm, D_MODEL), lambda i: (prev(i), 0))],
        out_shape=[jax.ShapeDtypeStruct((t, D_MODEL), F32), jax.ShapeDtypeStruct((t, D_MODEL), BF16)],
        scratch_shapes=[pltpu.VMEM((2, TOP_K * tm * SLAB_PITCH, LANES), F32), pltpu.SemaphoreType.DMA((2,))],
        compiler_params=_cparams(("arbitrary",)),
        name="combine_ln",
    )(dest_tiles, x, gate_tk, y_rows, wsg_bf, wsu_bf, wsd_bf, ln_g, ln_b)


def _dispatch_tables(top_e, rank, counts, blk, tm):
    k, t = top_e.shape
    n_blocks = (k * t) // blk + N_EXPERTS
    counts = counts[:, 0].astype(jnp.int32)
    padded = (counts + blk - 1) // blk * blk
    pend = jnp.cumsum(padded)
    pstart = pend - padded
    experts = jnp.arange(N_EXPERTS, dtype=jnp.int32)
    first_row = jnp.sum(jnp.where(top_e[None] == experts[:, None, None], pstart[:, None, None], 0), axis=0)
    dest = first_row + rank
    tok = jnp.broadcast_to(jnp.arange(t, dtype=jnp.int32)[None, :], (k, t))
    row_tok = jnp.zeros((n_blocks * blk,), jnp.int32).at[dest.reshape(-1)].set(
        tok.reshape(-1), unique_indices=True, indices_are_sorted=False)
    n_used = (pend[-1] // blk).astype(jnp.int32)
    starts = jnp.minimum(jnp.arange(n_blocks, dtype=jnp.int32), n_used - 1) * blk
    block_e = jnp.minimum(jnp.sum((pend[None, :] <= starts[:, None]).astype(jnp.int32), axis=1), N_EXPERTS - 1)
    dest_tiles = dest.reshape(k, t // tm, tm).transpose(1, 0, 2).reshape(t // tm, 1, k * tm)
    return row_tok.reshape(n_blocks, 1, blk), block_e, n_used.reshape(1), dest_tiles


def kernel(x, rel_bias, w_in, w_out, conv_w, conv_b, rg_wa, rg_ba, rg_wx, rg_bx, rg_lambda, rnn_norm_g,
           lambda_q1, lambda_k1, lambda_q2, lambda_k2, subln_g, ln1_g, ln1_b, w_router, b_router,
           w_exp_gate, w_exp_up, w_exp_down, w_sh_gate, w_sh_up, w_sh_down, ln2_g, ln2_b):
    bsz, seq, d = x.shape
    depth = w_in.shape[0]
    t = bsz * seq
    blk = min(MOE_BLOCK, t)
    ctile = min(COMBINE_TILE, t)

    w_in_bf = w_in.astype(BF16)
    w_out_bf = w_out.astype(BF16)
    wsg_bf, wsu_bf, wsd_bf = w_sh_gate.astype(BF16), w_sh_up.astype(BF16), w_sh_down.astype(BF16)
    w_router_t = jnp.swapaxes(w_router, 1, 2)
    b_router_col = jnp.broadcast_to(b_router[:, :, None], (depth, N_EXPERTS, LANES))

    def pair_blocks(w):
        w = w.reshape(depth, D_RNN // LANES, 2, RNN_BLOCK, RNN_BLOCK)
        z = jnp.zeros_like(w[:, :, 0])
        top = jnp.concatenate([w[:, :, 0], z], axis=-1)
        bot = jnp.concatenate([z, w[:, :, 1]], axis=-1)
        return jnp.concatenate([top, bot], axis=-2).astype(BF16)

    wa_bd, wx_bd = pair_blocks(rg_wa), pair_blocks(rg_wx)
    row3 = lambda a: a[:, None, :]
    lam4 = jnp.stack([lambda_q1, lambda_k1, lambda_q2, lambda_k2], axis=1)
    bias_tiles, far_bias = _bias_tables(rel_bias, min(ATT_TILE, seq))

    xt = x.reshape(t, d)
    xt_bf = xt.astype(BF16)
    for l in range(depth):
        lam_init = 0.8 - 0.6 * math.exp(-0.3 * l)
        proj = _in_proj(xt_bf, w_in_bf, l, 0, 2 * D_RNN + D_ATT, F32)
        kv = _in_proj(xt_bf, w_in_bf, l, 2 * D_RNN + D_ATT, 2 * D_ATT, BF16)
        rnn_out = _rglru(proj, bsz, seq, l, conv_w, row3(conv_b), wa_bd, row3(rg_ba), wx_bd, row3(rg_bx),
                         row3(rg_lambda), row3(rnn_norm_g))
        att_out = _attention(proj, kv, bsz, seq, l, far_bias, bias_tiles, lam4, subln_g[:, :, None], lam_init)
        xt = _out_proj_ln(xt, rnn_out, att_out, w_out_bf, row3(ln1_g), row3(ln1_b), l)
        top_e, gate, rank, counts = _router(xt, w_router_t, b_router_col, l)
        row_tok, block_e, n_used, dest_tiles = _dispatch_tables(top_e, rank, counts, blk, ctile)
        y_rows = _experts(xt.reshape(t, SLAB_ROWS, LANES), row_tok, block_e, n_used,
                          w_exp_gate, w_exp_up, w_exp_down, l, blk)
        xt, xt_bf = _combine(xt, gate.T, dest_tiles, y_rows, wsg_bf, wsu_bf, wsd_bf, row3(ln2_g), row3(ln2_b),
                             l, ctile)
    return xt.reshape(bsz, seq, d)
```
